```python
import jax, jax.numpy as jnp
from jax import lax
import numpy as np

D_MODEL = 2048
BATCH = 1
SEQ = 16384
DEPTH = 1

HEAD_DIM = 128
N_ATT_HEADS = 8
N_KV_GROUPS = 2
HEADS_PER_GROUP = N_ATT_HEADS // N_KV_GROUPS
ATT_WIDTH = N_ATT_HEADS * HEAD_DIM
KV_WIDTH = N_KV_GROUPS * HEAD_DIM
N_CONV_GROUPS = 8
CONV_WIDTH = N_CONV_GROUPS * HEAD_DIM
MIX_WIDTH = ATT_WIDTH + CONV_WIDTH
N_MIX_GROUPS = N_ATT_HEADS + N_CONV_GROUPS
N_BRANCH = 3
CMP_BLOCK = 32
CMP_STRIDE = 16
CMP_HIDDEN = 512
SLC_BLOCK = 64
N_SELECT = 16
WINDOW = 512
Q_BLOCK = 128
CONV_K = 3
PEER_HEADS = 8
N_KEYS = 128
N_EXPERTS = N_KEYS * N_KEYS
PEER_QDIM = 256
PEER_HALF = PEER_QDIM // 2
PEER_TOPK = 16
TOKEN_CHUNK = 128
ALPHA = (2.0 * DEPTH) ** 0.25
BETA = (8.0 * DEPTH) ** -0.25
LN_EPS = 1e-5
RMS_EPS = 1e-6
NEG_INF = -1e30
FORCE_BONUS = 1e4
SPLIT_SIZES = (ATT_WIDTH, KV_WIDTH, KV_WIDTH, KV_WIDTH, KV_WIDTH, KV_WIDTH, KV_WIDTH,
               N_ATT_HEADS * N_BRANCH, CONV_WIDTH, CONV_WIDTH, CONV_WIDTH)
IN_WIDTH = sum(SPLIT_SIZES)

kernel_name = 'hybrid_nsa_shortconv_peer_deepnorm'


def _layernorm(x, g, b):
    xf = x.astype(jnp.float32)
    mu = xf.mean(-1, keepdims=True)
    var = jnp.square(xf - mu).mean(-1, keepdims=True)
    return ((xf - mu) * lax.rsqrt(var + LN_EPS) * g + b).astype(x.dtype)


def _head_rmsnorm(y, g):
    B, S, _ = y.shape
    yf = y.astype(jnp.float32).reshape(B, S, N_MIX_GROUPS, HEAD_DIM)
    yf = yf * lax.rsqrt(jnp.mean(jnp.square(yf), -1, keepdims=True) + RMS_EPS)
    return (yf.reshape(B, S, MIX_WIDTH) * g).astype(y.dtype)


def _alibi_slopes():
    return 2.0 ** (-8.0 * jnp.arange(1, N_ATT_HEADS + 1, dtype=jnp.float32) / N_ATT_HEADS)


def _compress(kv, pos, w1, b1, w2):
    B, S, G, D = kv.shape
    n_cmp = (S - CMP_BLOCK) // CMP_STRIDE + 1
    idx = jnp.arange(n_cmp)[:, None] * CMP_STRIDE + jnp.arange(CMP_BLOCK)[None, :]
    blocks = kv[:, idx] + pos[None, None, :, None, :]
    blocks = blocks.transpose(0, 1, 3, 2, 4).reshape(B, n_cmp, G, CMP_BLOCK * D)
    return jax.nn.gelu(blocks @ w1 + b1) @ w2


def _nsa(q, k_cmp, v_cmp, k_slc, v_slc, k_win, v_win, gates):
    B, S = q.shape[:2]
    G, R, D = N_KV_GROUPS, HEADS_PER_GROUP, HEAD_DIM
    f32 = jnp.float32
    n_cmp = k_cmp.shape[1]
    n_slc = S // SLC_BLOCK
    n_sel = min(N_SELECT, n_slc)
    n_qb = S // Q_BLOCK
    scale = D ** -0.5
    slopes = _alibi_slopes().reshape(G, R)
    sl5 = slopes[None, :, :, None, None]
    sl6 = slopes[None, :, :, None, None, None]
    q = q.astype(f32)
    k_cmp, v_cmp = k_cmp.astype(f32), v_cmp.astype(f32)
    cmp_pos = jnp.arange(n_cmp, dtype=f32) * CMP_STRIDE + (CMP_BLOCK - 1) / 2.0
    cmp_end = jnp.arange(n_cmp) * CMP_STRIDE + CMP_BLOCK - 1
    ci = jnp.arange(n_cmp, dtype=f32)[:, None] * CMP_STRIDE
    sj = jnp.arange(n_slc, dtype=f32)[None, :] * SLC_BLOCK
    overlap = ((ci < sj + SLC_BLOCK) & (ci + CMP_BLOCK > sj)).astype(f32)
    k_blocks = k_slc.astype(f32).reshape(B, n_slc, SLC_BLOCK, G, D).transpose(0, 3, 1, 2, 4)
    v_blocks = v_slc.astype(f32).reshape(B, n_slc, SLC_BLOCK, G, D).transpose(0, 3, 1, 2, 4)
    k_wpad = jnp.pad(k_win.astype(f32), ((0, 0), (WINDOW, 0), (0, 0), (0, 0)))
    v_wpad = jnp.pad(v_win.astype(f32), ((0, 0), (WINDOW, 0), (0, 0), (0, 0)))
    b_ix = jnp.arange(B)[:, None, None, None]
    g_ix = jnp.arange(G)[None, :, None, None]
    jb = jnp.arange(n_slc)

    def block(nb):
        s0 = nb * Q_BLOCK
        t = s0 + jnp.arange(Q_BLOCK)
        tf = t.astype(f32)
        qb = lax.dynamic_slice_in_dim(q, s0, Q_BLOCK, axis=1).reshape(B, Q_BLOCK, G, R, D) * scale
        gb = lax.dynamic_slice_in_dim(gates, s0, Q_BLOCK, axis=1).reshape(B, Q_BLOCK, G, R, N_BRANCH)
        s = jnp.einsum('bqgrd,bngd->bgrqn', qb, k_cmp) - sl5 * (tf[:, None] - cmp_pos[None, :])
        valid = cmp_end[None, :] <= t[:, None]
        p_cmp = jax.nn.softmax(jnp.where(valid, s, NEG_INF), axis=-1) * valid
        o_cmp = jnp.einsum('bgrqn,bngd->bqgrd', p_cmp, v_cmp)
        imp = jnp.einsum('bgrqn,nj->bgqj', p_cmp, overlap)
        cur = t // SLC_BLOCK
        valid_s = jb[None, :] * SLC_BLOCK <= t[:, None]
        forced = (jb[None, :] == 0) | (jb[None, :] == cur[:, None]) | (jb[None, :] == cur[:, None] - 1)
        score = jnp.where(valid_s, imp + FORCE_BONUS * forced, -1.0)
        _, sel = lax.top_k(score, n_sel)
        kb = k_blocks[b_ix, g_ix, sel]
        vb = v_blocks[b_ix, g_ix, sel]
        pos = sel[..., None] * SLC_BLOCK + jnp.arange(SLC_BLOCK)
        dist = (t[:, None, None] - pos)[:, :, None]
        s = jnp.einsum('bqgrd,bgqkpd->bgrqkp', qb, kb) - sl6 * dist.astype(f32)
        s = jnp.where(dist >= 0, s, NEG_INF)
        p = jax.nn.softmax(s.reshape(B, G, R, Q_BLOCK, n_sel * SLC_BLOCK), axis=-1)
        p = p.reshape(B, G, R, Q_BLOCK, n_sel, SLC_BLOCK)
        o_slc = jnp.einsum('bgrqkp,bgqkpd->bqgrd', p, vb)
        kw = lax.dynamic_slice_in_dim(k_wpad, s0, WINDOW + Q_BLOCK, axis=1)
        vw = lax.dynamic_slice_in_dim(v_wpad, s0, WINDOW + Q_BLOCK, axis=1)
        posw = s0 - WINDOW + jnp.arange(WINDOW + Q_BLOCK)
        dw = t[:, None] - posw[None, :]
        mw = (dw >= 0) & (dw < WINDOW) & (posw[None, :] >= 0)
        s = jnp.einsum('bqgrd,bkgd->bgrqk', qb, kw) - sl5 * dw.astype(f32)
        p = jax.nn.softmax(jnp.where(mw, s, NEG_INF), axis=-1)
        o_win = jnp.einsum('bgrqk,bkgd->bqgrd', p, vw)
        o = gb[..., 0:1] * o_cmp + gb[..., 1:2] * o_slc + gb[..., 2:3] * o_win
        return o.reshape(B, Q_BLOCK, ATT_WIDTH)

    out = lax.map(block, jnp.arange(n_qb))
    return out.transpose(1, 0, 2, 3).reshape(B, S, ATT_WIDTH)


def _short_conv(u, gate_b, gate_c, conv_w):
    y = lax.conv_general_dilated(gate_c * u, conv_w[:, None, :].astype(u.dtype), window_strides=(1,),
                                 padding=[(CONV_K - 1, 0)], dimension_numbers=('NWC', 'WIO', 'NWC'),
                                 feature_group_count=CONV_WIDTH)
    return gate_b * y


def _peer(h, w_query, sub_keys, expert_u, expert_v):
    B, S, D = h.shape
    q = (h @ w_query).reshape(B, S, PEER_HEADS, 2, PEER_HALF).astype(jnp.float32)
    s = jnp.einsum('bshcd,hckd->bshck', q, sub_keys.astype(jnp.float32))
    top_s, top_i = lax.top_k(s, PEER_TOPK)
    cand = top_s[..., 0, :, None] + top_s[..., 1, None, :]
    cand = cand.reshape(B, S, PEER_HEADS, PEER_TOPK * PEER_TOPK)
    best_s, best_i = lax.top_k(cand, PEER_TOPK)
    i1 = jnp.take_along_axis(top_i[..., 0, :], best_i // PEER_TOPK, axis=-1)
    i2 = jnp.take_along_axis(top_i[..., 1, :], best_i % PEER_TOPK, axis=-1)
    eid = i1 * N_KEYS + i2
    g = jax.nn.softmax(best_s, axis=-1).astype(h.dtype)
    n_ch = (B * S) // TOKEN_CHUNK
    hc = h.reshape(n_ch, TOKEN_CHUNK, D)
    ec = eid.reshape(n_ch, TOKEN_CHUNK, PEER_HEADS, PEER_TOPK)
    gc = g.reshape(n_ch, TOKEN_CHUNK, PEER_HEADS, PEER_TOPK)

    def chunk(args):
        hx, e, gw = args
        a = jax.nn.gelu(jnp.einsum('td,thkd->thk', hx, expert_u[e]))
        return jnp.einsum('thk,thkd->td', gw * a, expert_v[e])

    return lax.map(chunk, (hc, ec, gc)).reshape(B, S, D)


def setup_inputs(seed: int = 0) -> dict:
    key = jax.random.key(seed)
    ks = jax.random.split(key, 24)
    L, D = DEPTH, D_MODEL
    nrm = lambda k, shape, sc: jax.random.normal(k, shape, jnp.float32) * sc
    col_scale = np.concatenate([
        np.ones(ATT_WIDTH), np.ones(KV_WIDTH), np.full(KV_WIDTH, BETA), np.ones(KV_WIDTH),
        np.full(KV_WIDTH, BETA), np.ones(KV_WIDTH), np.full(KV_WIDTH, BETA),
        np.ones(N_ATT_HEADS * N_BRANCH), np.full(CONV_WIDTH, BETA), np.ones(CONV_WIDTH),
        np.ones(CONV_WIDTH)]).astype(np.float32)
    return {
        'x': nrm(ks[0], (BATCH, SEQ, D), 1.0),
        'w_in': nrm(ks[1], (L, D, IN_WIDTH), D ** -0.5) * jnp.asarray(col_scale),
        'cmp_k_pos': nrm(ks[2], (L, CMP_BLOCK, HEAD_DIM), 0.1),
        'cmp_k_w1': nrm(ks[3], (L, CMP_BLOCK * HEAD_DIM, CMP_HIDDEN), (CMP_BLOCK * HEAD_DIM) ** -0.5),
        'cmp_k_b1': nrm(ks[4], (L, CMP_HIDDEN), 0.01),
        'cmp_k_w2': nrm(ks[5], (L, CMP_HIDDEN, HEAD_DIM), CMP_HIDDEN ** -0.5),
        'cmp_v_pos': nrm(ks[6], (L, CMP_BLOCK, HEAD_DIM), 0.1),
        'cmp_v_w1': nrm(ks[7], (L, CMP_BLOCK * HEAD_DIM, CMP_HIDDEN), (CMP_BLOCK * HEAD_DIM) ** -0.5),
        'cmp_v_b1': nrm(ks[8], (L, CMP_HIDDEN), 0.01),
        'cmp_v_w2': nrm(ks[9], (L, CMP_HIDDEN, HEAD_DIM), CMP_HIDDEN ** -0.5),
        'conv_w': nrm(ks[10], (L, CONV_K, CONV_WIDTH), 0.5),
        'head_norm_g': 1.0 + nrm(ks[11], (L, MIX_WIDTH), 0.02),
        'w_out': nrm(ks[12], (L, MIX_WIDTH, D), BETA * MIX_WIDTH ** -0.5),
        'ln1_g': 1.0 + nrm(ks[13], (L, D), 0.02),
        'ln1_b': nrm(ks[14], (L, D), 0.01),
        'w_query': nrm(ks[15], (L, D, PEER_HEADS * PEER_QDIM), D ** -0.5),
        'sub_keys': nrm(ks[16], (L, PEER_HEADS, 2, N_KEYS, PEER_HALF), PEER_HALF ** -0.5),
        'expert_u': nrm(ks[17], (L, N_EXPERTS, D), D ** -0.5),
        'expert_v': nrm(ks[18], (L, N_EXPERTS, D), BETA),
        'ln2_g': 1.0 + nrm(ks[19], (L, D), 0.02),
        'ln2_b': nrm(ks[20], (L, D), 0.01),
    }


def reference(x, w_in, cmp_k_pos, cmp_k_w1, cmp_k_b1, cmp_k_w2, cmp_v_pos, cmp_v_w1, cmp_v_b1,
              cmp_v_w2, conv_w, head_norm_g, w_out, ln1_g, ln1_b, w_query, sub_keys, expert_u,
              expert_v, ln2_g, ln2_b):
    B, S, _ = x.shape
    offs = np.cumsum(SPLIT_SIZES)[:-1].tolist()
    for l in range(DEPTH):
        proj = x @ w_in[l]
        q, k_c, v_c, k_s, v_s, k_w, v_w, gate_logits, u, gate_b, gate_c = jnp.split(proj, offs, axis=-1)
        grp = lambda t: t.reshape(B, S, N_KV_GROUPS, HEAD_DIM)
        kc = _compress(grp(k_c), cmp_k_pos[l], cmp_k_w1[l], cmp_k_b1[l], cmp_k_w2[l])
        vc = _compress(grp(v_c), cmp_v_pos[l], cmp_v_w1[l], cmp_v_b1[l], cmp_v_w2[l])
        gates = jax.nn.sigmoid(gate_logits).reshape(B, S, N_ATT_HEADS, N_BRANCH)
        o_att = _nsa(q.reshape(B, S, N_ATT_HEADS, HEAD_DIM), kc, vc, grp(k_s), grp(v_s),
                     grp(k_w), grp(v_w), gates).astype(x.dtype)
        o_conv = _short_conv(u, gate_b, gate_c, conv_w[l])
        mix = _head_rmsnorm(jnp.concatenate([o_att, o_conv], axis=-1), head_norm_g[l])
        h = _layernorm(ALPHA * x + mix @ w_out[l], ln1_g[l], ln1_b[l])
        x = _layernorm(ALPHA * h + _peer(h, w_query[l], sub_keys[l], expert_u[l], expert_v[l]),
                       ln2_g[l], ln2_b[l])
    return x
```

```python
import functools

import numpy as np
import jax
import jax.numpy as jnp
from jax import lax
from jax.experimental import pallas as pl
from jax.experimental.pallas import tpu as pltpu

D_MODEL = 2048
HEAD_DIM = 128
N_ATT_HEADS = 8
N_KV_GROUPS = 2
HEADS_PER_GROUP = N_ATT_HEADS // N_KV_GROUPS
ATT_WIDTH = N_ATT_HEADS * HEAD_DIM
KV_WIDTH = N_KV_GROUPS * HEAD_DIM
CONV_WIDTH = 1024
MIX_WIDTH = ATT_WIDTH + CONV_WIDTH
N_MIX_GROUPS = MIX_WIDTH // HEAD_DIM
N_BRANCH = 3
CMP_BLOCK = 32
CMP_STRIDE = 16
CMP_HIDDEN = 512
SLC_BLOCK = 64
N_SELECT = 16
WINDOW = 512
Q_BLOCK = 128
CONV_K = 3
PEER_HEADS = 8
N_KEYS = 128
PEER_HALF = 128
PEER_TOPK = 16
PEER_K = PEER_HEADS * PEER_TOPK
DEPTH = 1
ALPHA = (2.0 * DEPTH) ** 0.25
LN_EPS = 1e-5
RMS_EPS = 1e-6
NEG_INF = -1e30
FORCE_BONUS = 1e4

LANES = 128
SUBLANES = 8
VMEM_LIMIT_BYTES = 56 * 1024 * 1024

GATE_PAD = LANES
OFF_Q = 0
OFF_U = OFF_Q + ATT_WIDTH
OFF_GB = OFF_U + CONV_WIDTH
OFF_GC = OFF_GB + CONV_WIDTH
OFF_KC = OFF_GC + CONV_WIDTH
OFF_VC = OFF_KC + KV_WIDTH
OFF_KS = OFF_VC + KV_WIDTH
OFF_VS = OFF_KS + KV_WIDTH
OFF_KW = OFF_VS + KV_WIDTH
OFF_VW = OFF_KW + KV_WIDTH
OFF_GATE = OFF_VW + KV_WIDTH
PROJ_W = OFF_GATE + GATE_PAD

SLC_TILE = 512
SLC_PER_TILE = SLC_TILE // SLC_BLOCK
WIN_KEYS = WINDOW + Q_BLOCK
WIN_BLOCKS = WIN_KEYS // Q_BLOCK

PEER_TB = 8
PEER_ROWS = PEER_TB * PEER_K
ISSUE_UNROLL = 32

_BF16 = jnp.bfloat16
_F32 = jnp.float32
_NT = (((1,), (1,)), ((), ()))


def _cparams(sem):
    return pltpu.CompilerParams(dimension_semantics=sem, vmem_limit_bytes=VMEM_LIMIT_BYTES)


def _matmul_kernel(x_ref, w_ref, o_ref):
    o_ref[...] = jnp.dot(x_ref[...], w_ref[...], preferred_element_type=_F32)


def in_proj(xb, wb):
    S, K = xb.shape
    N = wb.shape[1]
    tm = min(512, S)
    tn = N // 5
    return pl.pallas_call(
        _matmul_kernel,
        grid=(N // tn, S // tm),
        in_specs=[pl.BlockSpec((tm, K), lambda j, i: (i, 0)),
                  pl.BlockSpec((K, tn), lambda j, i: (0, j))],
        out_specs=pl.BlockSpec((tm, tn), lambda j, i: (i, j)),
        out_shape=jax.ShapeDtypeStruct((S, N), _F32),
        compiler_params=_cparams(("arbitrary", "arbitrary")),
        name="in_proj",
    )(xb, wb)


def _compress_kernel(hb_ref, pos_ref, w1_ref, b1_ref, w2_ref, o_ref):
    nh = hb_ref.shape[1]
    half = hb_ref.shape[2]
    hb = hb_ref[0]
    top = (hb + pos_ref[:, :half]).astype(_BF16)
    bot = (hb + pos_ref[:, half:]).astype(_BF16)
    a = jnp.dot(top, w1_ref[:half, :], preferred_element_type=_F32)
    b = jnp.dot(bot, w1_ref[half:, :], preferred_element_type=_F32)
    hidden = a + pltpu.roll(b, nh - 1, 0) + b1_ref[...]
    act = jax.nn.gelu(hidden).astype(_BF16)
    o_ref[0] = jnp.dot(act, w2_ref[...], preferred_element_type=_F32)


def compress(hb, pos, w1, b1, w2):
    G, nh, half = hb.shape
    posflat = pos.reshape(1, CMP_BLOCK * HEAD_DIM)
    return pl.pallas_call(
        _compress_kernel,
        grid=(G,),
        in_specs=[pl.BlockSpec((1, nh, half), lambda g: (g, 0, 0)),
                  pl.BlockSpec((1, 2 * half), lambda g: (0, 0)),
                  pl.BlockSpec((2 * half, CMP_HIDDEN), lambda g: (0, 0)),
                  pl.BlockSpec((1, CMP_HIDDEN), lambda g: (0, 0)),
                  pl.BlockSpec((CMP_HIDDEN, HEAD_DIM), lambda g: (0, 0))],
        out_specs=pl.BlockSpec((1, nh, HEAD_DIM), lambda g: (g, 0, 0)),
        out_shape=jax.ShapeDtypeStruct((G, nh, HEAD_DIM), _F32),
        compiler_params=_cparams(("arbitrary",)),
        name="compress",
    )(hb, posflat, w1.astype(_BF16), b1.reshape(1, CMP_HIDDEN), w2.astype(_BF16))


def _nsa_kernel(q_ref, kc_ref, vct_ref, ks_ref, vst_ref, kw_ref, vwt_ref, gl_ref, slope_ref, ovt_ref,
                o_ref, sel_ref, m_ref, l_ref, acc_ref, *, n_sel):
    nb = pl.program_id(1)
    s0 = nb * Q_BLOCK
    R = HEADS_PER_GROUP
    L = R * Q_BLOCK
    nh = kc_ref.shape[1]
    n_slc = ovt_ref.shape[0]
    scale = HEAD_DIM ** -0.5

    qs = jnp.concatenate([q_ref[:, r * HEAD_DIM:(r + 1) * HEAD_DIM] for r in range(R)], axis=0)
    qs = (qs * scale).astype(_BF16)
    slope = slope_ref[0]
    lane = lax.broadcasted_iota(jnp.int32, (1, L), 1)
    t_i = s0 + (lane & (Q_BLOCK - 1))
    t_f = t_i.astype(_F32)

    sc = lax.dot_general(kc_ref[0], qs, _NT, preferred_element_type=_F32)
    n_io = lax.broadcasted_iota(jnp.int32, (nh, L), 0)
    valid = (n_io * CMP_STRIDE + (CMP_BLOCK - 1)) <= t_i
    cpos = n_io.astype(_F32) * CMP_STRIDE + (CMP_BLOCK - 1) / 2.0
    s = jnp.where(valid, sc - slope * (t_f - cpos), NEG_INF)
    m = jnp.max(s, axis=0, keepdims=True)
    e = jnp.where(valid, jnp.exp(s - m), 0.0)
    l = jnp.sum(e, axis=0, keepdims=True)
    p = e * (1.0 / jnp.maximum(l, 1e-30))
    o_cmp = jnp.dot(vct_ref[0], p.astype(_BF16), preferred_element_type=_F32)

    ps = p[:, 0:Q_BLOCK]
    for r in range(1, R):
        ps = ps + p[:, r * Q_BLOCK:(r + 1) * Q_BLOCK]
    hi = ps.astype(_BF16)
    r1 = ps - hi.astype(_F32)
    mid = r1.astype(_BF16)
    lo = (r1 - mid.astype(_F32)).astype(_BF16)
    ovt = ovt_ref[...]
    imp = (jnp.dot(ovt, hi, preferred_element_type=_F32) + jnp.dot(ovt, mid, preferred_element_type=_F32)
           + jnp.dot(ovt, lo, preferred_element_type=_F32))

    j_io = lax.broadcasted_iota(jnp.int32, (n_slc, Q_BLOCK), 0)
    tq = s0 + lax.broadcasted_iota(jnp.int32, (1, Q_BLOCK), 1)
    cur = tq >> 6
    valid_s = (j_io * SLC_BLOCK) <= tq
    forced = (j_io == 0) | (j_io == cur) | (j_io == cur - 1)
    score = jnp.where(valid_s, imp + jnp.where(forced, FORCE_BONUS, 0.0), -1.0)

    def pick(_, carry):
        work, sel = carry
        mx = jnp.max(work, axis=0, keepdims=True)
        first = jnp.min(jnp.where(work == mx, j_io, n_slc), axis=0, keepdims=True)
        hit = j_io == first
        return jnp.where(hit, -jnp.inf, work), jnp.where(hit, 1.0, sel)

    _, sel = lax.fori_loop(0, n_sel, pick, (score, jnp.zeros((n_slc, Q_BLOCK), _F32)))
    sel_ref[...] = sel

    m_ref[...] = jnp.full((1, L), NEG_INF, _F32)
    l_ref[...] = jnp.zeros((1, L), _F32)
    acc_ref[...] = jnp.zeros((HEAD_DIM, L), _F32)
    key_io = lax.broadcasted_iota(jnp.int32, (SLC_TILE, L), 0)

    def slc_tile(kt, carry):
        selt = sel_ref[pl.ds(pl.multiple_of(kt * SLC_PER_TILE, SLC_PER_TILE), SLC_PER_TILE), :]

        @pl.when(jnp.max(selt) > 0.0)
        def _():
            sx = jnp.broadcast_to(selt[:, None, :], (SLC_PER_TILE, SLC_BLOCK, Q_BLOCK)).reshape(SLC_TILE, Q_BLOCK)
            sx = jnp.concatenate([sx] * R, axis=1)
            dist = t_i - (kt * SLC_TILE + key_io)
            ok = (sx > 0.0) & (dist >= 0)
            sr = lax.dot_general(ks_ref[0, kt], qs, _NT, preferred_element_type=_F32)
            st = jnp.where(ok, sr - slope * dist.astype(_F32), NEG_INF)
            m_old = m_ref[...]
            m_new = jnp.maximum(m_old, jnp.max(st, axis=0, keepdims=True))
            alpha = jnp.exp(m_old - m_new)
            et = jnp.where(ok, jnp.exp(st - m_new), 0.0)
            l_ref[...] = alpha * l_ref[...] + jnp.sum(et, axis=0, keepdims=True)
            acc_ref[...] = alpha * acc_ref[...] + jnp.dot(vst_ref[0, kt], et.astype(_BF16),
                                                          preferred_element_type=_F32)
            m_ref[...] = m_new
        return carry

    lax.fori_loop(0, (s0 + Q_BLOCK + SLC_TILE - 1) // SLC_TILE, slc_tile, 0)
    o_slc = acc_ref[...] * (1.0 / l_ref[...])

    start = pl.multiple_of(jnp.maximum(s0 - WINDOW, 0), Q_BLOCK)
    sw = lax.dot_general(kw_ref[0, pl.ds(start, WIN_KEYS), :], qs, _NT, preferred_element_type=_F32)
    dw = t_i - (start + lax.broadcasted_iota(jnp.int32, (WIN_KEYS, L), 0))
    okw = (dw >= 0) & (dw < WINDOW)
    s = jnp.where(okw, sw - slope * dw.astype(_F32), NEG_INF)
    m = jnp.max(s, axis=0, keepdims=True)
    e = jnp.where(okw, jnp.exp(s - m), 0.0)
    pw = (e * (1.0 / jnp.sum(e, axis=0, keepdims=True))).astype(_BF16)
    b0 = start // Q_BLOCK
    o_win = jnp.dot(vwt_ref[0, b0], pw[0:Q_BLOCK], preferred_element_type=_F32)
    for i in range(1, WIN_BLOCKS):
        o_win = o_win + jnp.dot(vwt_ref[0, b0 + i], pw[i * Q_BLOCK:(i + 1) * Q_BLOCK],
                                preferred_element_type=_F32)

    sg = jax.nn.sigmoid(gl_ref[0, 0])
    o = sg[0:1] * o_cmp + sg[1:2] * o_slc + sg[2:3] * o_win
    for r in range(R):
        o_ref[:, r * HEAD_DIM:(r + 1) * HEAD_DIM] = o[:, r * Q_BLOCK:(r + 1) * Q_BLOCK].T


def nsa(proj, kc, vc):
    S = proj.shape[0]
    G, R = N_KV_GROUPS, HEADS_PER_GROUP
    nh = S // CMP_STRIDE
    n_slc = S // SLC_BLOCK
    n_qb = S // Q_BLOCK
    n_kt = S // SLC_TILE
    n_sel = min(N_SELECT, n_slc)
    L = R * Q_BLOCK

    def grp(off):
        return proj[:, off:off + KV_WIDTH].reshape(S, G, HEAD_DIM).transpose(1, 0, 2)

    kcb = kc.astype(_BF16)
    vct = vc.transpose(0, 2, 1).astype(_BF16)
    ks3 = grp(OFF_KS).astype(_BF16).reshape(G, n_kt, SLC_TILE, HEAD_DIM)
    vst3 = grp(OFF_VS).astype(_BF16).reshape(G, n_kt, SLC_TILE, HEAD_DIM).transpose(0, 1, 3, 2)
    kw = grp(OFF_KW).astype(_BF16)
    vwt3 = grp(OFF_VW).astype(_BF16).reshape(G, n_qb, Q_BLOCK, HEAD_DIM).transpose(0, 1, 3, 2)
    gl = proj[:, OFF_GATE:OFF_GATE + N_ATT_HEADS * N_BRANCH].reshape(n_qb, Q_BLOCK, G, R, N_BRANCH)
    gl = gl.transpose(2, 0, 4, 3, 1).reshape(G, n_qb, N_BRANCH, L)
    head = np.arange(N_ATT_HEADS, dtype=np.float64).reshape(G, R)
    slopes = np.repeat(2.0 ** (-8.0 * (head + 1) / N_ATT_HEADS), Q_BLOCK, axis=1).reshape(G, 1, L)
    ci = np.arange(nh)[None, :] * CMP_STRIDE
    sj = np.arange(n_slc)[:, None] * SLC_BLOCK
    ovt = ((ci < sj + SLC_BLOCK) & (ci + CMP_BLOCK > sj) & (np.arange(nh)[None, :] < nh - 1))

    return pl.pallas_call(
        functools.partial(_nsa_kernel, n_sel=n_sel),
        grid=(G, n_qb),
        in_specs=[
            pl.BlockSpec((Q_BLOCK, L), lambda g, i: (i, g)),
            pl.BlockSpec((1, nh, HEAD_DIM), lambda g, i: (g, 0, 0)),
            pl.BlockSpec((1, HEAD_DIM, nh), lambda g, i: (g, 0, 0)),
            pl.BlockSpec((1, n_kt, SLC_TILE, HEAD_DIM), lambda g, i: (g, 0, 0, 0)),
            pl.BlockSpec((1, n_kt, HEAD_DIM, SLC_TILE), lambda g, i: (g, 0, 0, 0)),
            pl.BlockSpec((1, S, HEAD_DIM), lambda g, i: (g, 0, 0)),
            pl.BlockSpec((1, n_qb, HEAD_DIM, Q_BLOCK), lambda g, i: (g, 0, 0, 0)),
            pl.BlockSpec((1, 1, N_BRANCH, L), lambda g, i: (g, i, 0, 0)),
            pl.BlockSpec((1, 1, L), lambda g, i: (g, 0, 0)),
            pl.BlockSpec((n_slc, nh), lambda g, i: (0, 0)),
        ],
        out_specs=pl.BlockSpec((Q_BLOCK, L), lambda g, i: (i, g)),
        out_shape=jax.ShapeDtypeStruct((S, ATT_WIDTH), _F32),
        scratch_shapes=[pltpu.VMEM((n_slc, Q_BLOCK), _F32),
                        pltpu.VMEM((1, L), _F32),
                        pltpu.VMEM((1, L), _F32),
                        pltpu.VMEM((HEAD_DIM, L), _F32)],
        compiler_params=_cparams(("arbitrary", "arbitrary")),
        name="nsa",
    )(proj, kcb, vct, ks3, vst3, kw, vwt3, gl, jnp.asarray(slopes, _F32), jnp.asarray(ovt, _BF16))


def _layernorm_rows(r, g, b):
    mu = jnp.mean(r, axis=-1, keepdims=True)
    c = r - mu
    var = jnp.mean(c * c, axis=-1, keepdims=True)
    return c * lax.rsqrt(var + LN_EPS) * g + b


def _mix_out_kernel(x_ref, att_ref, u_ref, gb_ref, gc_ref, uh_ref, gch_ref, cw_ref, hg_ref, wo_ref,
                    g1_ref, b1_ref, h_ref):
    i = pl.program_id(0)
    tm = x_ref.shape[0]
    z = gc_ref[...] * u_ref[...]
    zh = jnp.where(i > 0, gch_ref[...] * uh_ref[...], 0.0)
    zz = jnp.concatenate([zh, z], axis=0)
    z1 = pltpu.roll(zz, 1, 0)[SUBLANES:]
    z2 = pltpu.roll(zz, 2, 0)[SUBLANES:]
    y = cw_ref[0:1, :] * z2 + cw_ref[1:2, :] * z1 + cw_ref[2:3, :] * z
    o_conv = gb_ref[...] * y
    mix = jnp.concatenate([att_ref[...], o_conv], axis=1)
    parts = []
    for k in range(N_MIX_GROUPS):
        blk = mix[:, k * HEAD_DIM:(k + 1) * HEAD_DIM]
        ms = jnp.mean(blk * blk, axis=-1, keepdims=True)
        parts.append(blk * lax.rsqrt(ms + RMS_EPS))
    mixn = (jnp.concatenate(parts, axis=1) * hg_ref[...]).astype(_BF16)
    r = ALPHA * x_ref[...] + jnp.dot(mixn, wo_ref[...], preferred_element_type=_F32)
    h_ref[...] = _layernorm_rows(r, g1_ref[...], b1_ref[...])


def mix_out(x2, o_att, proj, conv_w, head_norm_g, w_out, ln_g, ln_b):
    S = x2.shape[0]
    tm = min(256, S)
    hb = tm // SUBLANES
    cblk = lambda off: off // CONV_WIDTH
    halo = lambda off: pl.BlockSpec((SUBLANES, CONV_WIDTH), lambda i: (jnp.maximum(i * hb - 1, 0), cblk(off)))
    tile = lambda off: pl.BlockSpec((tm, CONV_WIDTH), lambda i: (i, cblk(off)))
    full = lambda shape: pl.BlockSpec(shape, lambda i: (0, 0))
    cw = jnp.zeros((SUBLANES, CONV_WIDTH), _F32).at[:CONV_K].set(conv_w)
    return pl.pallas_call(
        _mix_out_kernel,
        grid=(S // tm,),
        in_specs=[pl.BlockSpec((tm, D_MODEL), lambda i: (i, 0)),
                  pl.BlockSpec((tm, ATT_WIDTH), lambda i: (i, 0)),
                  tile(OFF_U), tile(OFF_GB), tile(OFF_GC), halo(OFF_U), halo(OFF_GC),
                  full((SUBLANES, CONV_WIDTH)), full((1, MIX_WIDTH)), full((MIX_WIDTH, D_MODEL)),
                  full((1, D_MODEL)), full((1, D_MODEL))],
        out_specs=pl.BlockSpec((tm, D_MODEL), lambda i: (i, 0)),
        out_shape=jax.ShapeDtypeStruct((S, D_MODEL), _F32),
        compiler_params=_cparams(("arbitrary",)),
        name="mix_out",
    )(x2, o_att, proj, proj, proj, proj, proj, cw, head_norm_g.reshape(1, MIX_WIDTH), w_out.astype(_BF16),
      ln_g.reshape(1, D_MODEL), ln_b.reshape(1, D_MODEL))


def _topk_rows(x, k):
    R, L = x.shape
    io = lax.broadcasted_iota(jnp.int32, (R, L), 0)
    ko = lax.broadcasted_iota(jnp.int32, (k, L), 0)
    vals = jnp.zeros((k, L), _F32)
    idxs = jnp.zeros((k, L), jnp.int32)
    for i in range(k):
        mx = jnp.max(x, axis=0, keepdims=True)
        first = jnp.min(jnp.where(x == mx, io, R), axis=0, keepdims=True)
        x = jnp.where(io == first, -jnp.inf, x)
        vals = jnp.where(ko == i, mx, vals)
        idxs = jnp.where(ko == i, first, idxs)
    return vals, idxs


def _take_rows(table, idx):
    out = jnp.zeros(idx.shape, table.dtype)
    for a in range(table.shape[0]):
        out = jnp.where(idx == a, table[a:a + 1, :], out)
    return out


def _peer_route_kernel(h_ref, wq_ref, sk_ref, eid_ref, g_ref):
    qp = jnp.dot(h_ref[...].astype(_BF16), wq_ref[...], preferred_element_type=_F32)
    for hd in range(PEER_HEADS):
        tops = []
        for c in range(2):
            j = hd * 2 + c
            qhc = qp[:, j * PEER_HALF:(j + 1) * PEER_HALF].astype(_BF16)
            st = lax.dot_general(sk_ref[j], qhc, _NT, preferred_element_type=_F32)
            tops.append(_topk_rows(st, PEER_TOPK))
        (v1, i1), (v2, i2) = tops
        cand = jnp.concatenate([v1[a:a + 1, :] + v2 for a in range(PEER_TOPK)], axis=0)
        bs, bi = _topk_rows(cand, PEER_TOPK)
        e1 = _take_rows(i1, bi >> 4)
        e2 = _take_rows(i2, bi & (PEER_TOPK - 1))
        ex = jnp.exp(bs - jnp.max(bs, axis=0, keepdims=True))
        eid_ref[hd * PEER_TOPK:(hd + 1) * PEER_TOPK, :] = e1 * N_KEYS + e2
        g_ref[hd * PEER_TOPK:(hd + 1) * PEER_TOPK, :] = ex * (1.0 / jnp.sum(ex, axis=0, keepdims=True))


def peer_route(h, w_query, sub_keys):
    S = h.shape[0]
    tt = LANES
    skb = sub_keys.reshape(PEER_HEADS * 2, N_KEYS, PEER_HALF).astype(_BF16)
    return pl.pallas_call(
        _peer_route_kernel,
        grid=(S // tt,),
        in_specs=[pl.BlockSpec((tt, D_MODEL), lambda i: (i, 0)),
                  pl.BlockSpec((D_MODEL, PEER_HEADS * 2 * PEER_HALF), lambda i: (0, 0)),
                  pl.BlockSpec((PEER_HEADS * 2, N_KEYS, PEER_HALF), lambda i: (0, 0, 0))],
        out_specs=[pl.BlockSpec((PEER_K, tt), lambda i: (0, i)),
                   pl.BlockSpec((PEER_K, tt), lambda i: (0, i))],
        out_shape=[jax.ShapeDtypeStruct((PEER_K, S), jnp.int32),
                   jax.ShapeDtypeStruct((PEER_K, S), _F32)],
        compiler_params=_cparams(("arbitrary",)),
        name="peer_route",
    )(h, w_query.astype(_BF16), skb)


def _row_copy(uv_hbm, buf, sem, slot, e, r):
    return pltpu.make_async_copy(uv_hbm.at[pl.ds(e, 1)], buf.at[slot, pl.ds(r, 1)], sem.at[slot])


def _peer_apply_kernel(eid_cur_ref, eid_nxt_ref, h_ref, g_ref, g2_ref, b2_ref, uv_hbm, o_ref, buf, sem):
    i = pl.program_id(0)
    n = pl.num_programs(0)
    slot = lax.rem(i, 2)

    def issue(eid_ref, s):
        def body(c, carry):
            for j in range(ISSUE_UNROLL):
                r = c * ISSUE_UNROLL + j
                _row_copy(uv_hbm, buf, sem, s, eid_ref[0, 0, r], r).start()
            return carry
        lax.fori_loop(0, PEER_ROWS // ISSUE_UNROLL, body, 0)

    @pl.when(i == 0)
    def _():
        issue(eid_cur_ref, 0)

    @pl.when(i + 1 < n)
    def _():
        issue(eid_nxt_ref, 1 - slot)

    pltpu.make_async_copy(buf.at[slot], buf.at[slot], sem.at[slot]).wait()

    rows = buf[slot]
    u = rows[:, :D_MODEL].astype(_BF16)
    v = rows[:, D_MODEL:].astype(_BF16)
    h = h_ref[...]
    a = lax.dot_general(h.astype(_BF16), u, _NT, preferred_element_type=_F32)
    tok = lax.broadcasted_iota(jnp.int32, (PEER_TB, PEER_ROWS), 0)
    col = lax.broadcasted_iota(jnp.int32, (PEER_TB, PEER_ROWS), 1)
    own = (col >> 7) == tok
    gt = jnp.concatenate([g_ref[...]] * PEER_TB, axis=1)
    c = jnp.where(own, jax.nn.gelu(a) * gt, 0.0).astype(_BF16)
    peer = jnp.dot(c, v, preferred_element_type=_F32)
    o_ref[...] = _layernorm_rows(ALPHA * h + peer, g2_ref[...], b2_ref[...])


def peer_apply(h, eid, gates, uv, ln_g, ln_b):
    S = h.shape[0]
    nbat = S // PEER_TB
    eid3 = eid.reshape(nbat, 1, PEER_ROWS)
    return pl.pallas_call(
        _peer_apply_kernel,
        grid=(nbat,),
        in_specs=[
            pl.BlockSpec((1, 1, PEER_ROWS), lambda i: (i, 0, 0), memory_space=pltpu.SMEM),
            pl.BlockSpec((1, 1, PEER_ROWS), lambda i: (jnp.minimum(i + 1, nbat - 1), 0, 0),
                         memory_space=pltpu.SMEM),
            pl.BlockSpec((PEER_TB, D_MODEL), lambda i: (i, 0)),
            pl.BlockSpec((PEER_TB, PEER_K), lambda i: (i, 0)),
            pl.BlockSpec((1, D_MODEL), lambda i: (0, 0)),
            pl.BlockSpec((1, D_MODEL), lambda i: (0, 0)),
            pl.BlockSpec(memory_space=pl.ANY),
        ],
        out_specs=pl.BlockSpec((PEER_TB, D_MODEL), lambda i: (i, 0)),
        out_shape=jax.ShapeDtypeStruct((S, D_MODEL), _F32),
        scratch_shapes=[pltpu.VMEM((2, PEER_ROWS, 2 * D_MODEL), _F32),
                        pltpu.SemaphoreType.DMA((2,))],
        compiler_params=_cparams(("arbitrary",)),
        name="peer_apply",
    )(eid3, eid3, h, gates, ln_g.reshape(1, D_MODEL), ln_b.reshape(1, D_MODEL), uv)


def _pad_w_in(w):
    offs = np.cumsum([0, ATT_WIDTH] + [KV_WIDTH] * 6 + [N_ATT_HEADS * N_BRANCH] + [CONV_WIDTH] * 3)
    seg = lambda k: w[:, offs[k]:offs[k + 1]]
    q, kc, vc, ks, vs, kw, vw, gate, u, gb, gc = [seg(k) for k in range(11)]
    gate = jnp.pad(gate, ((0, 0), (0, GATE_PAD - gate.shape[1])))
    return jnp.concatenate([q, u, gb, gc, kc, vc, ks, vs, kw, vw, gate], axis=1)


def _layer(x2, w_in, cmp_k, cmp_v, conv_w, head_norm_g, w_out, ln1, w_query, sub_keys, expert_u, expert_v, ln2):
    S = x2.shape[0]
    G = N_KV_GROUPS
    proj = in_proj(x2.astype(_BF16), _pad_w_in(w_in).astype(_BF16))

    def half_blocks(off):
        t = proj[:, off:off + KV_WIDTH].reshape(S, G, HEAD_DIM).transpose(1, 0, 2)
        return t.reshape(G, S // CMP_STRIDE, CMP_STRIDE * HEAD_DIM)

    kc = compress(half_blocks(OFF_KC), *cmp_k)
    vc = compress(half_blocks(OFF_VC), *cmp_v)
    o_att = nsa(proj, kc, vc)
    h = mix_out(x2, o_att, proj, conv_w, head_norm_g, w_out, *ln1)
    eid_t, g_t = peer_route(h, w_query, sub_keys)
    uv = jnp.concatenate([expert_u, expert_v], axis=1)
    return peer_apply(h, eid_t.T, g_t.T, uv, *ln2)


def kernel(x, w_in, cmp_k_pos, cmp_k_w1, cmp_k_b1, cmp_k_w2, cmp_v_pos, cmp_v_w1, cmp_v_b1, cmp_v_w2, conv_w, head_norm_g, w_out, ln1_g, ln1_b, w_query, sub_keys, expert_u, expert_v, ln2_g, ln2_b):
    B, S, D = x.shape
    outs = []
    for b in range(B):
        xb = x[b]
        for l in range(w_in.shape[0]):
            xb = _layer(xb, w_in[l],
                        (cmp_k_pos[l], cmp_k_w1[l], cmp_k_b1[l], cmp_k_w2[l]),
                        (cmp_v_pos[l], cmp_v_w1[l], cmp_v_b1[l], cmp_v_w2[l]),
                        conv_w[l], head_norm_g[l], w_out[l], (ln1_g[l], ln1_b[l]),
                        w_query[l], sub_keys[l], expert_u[l], expert_v[l], (ln2_g[l], ln2_b[l]))
        outs.append(xb)
    return jnp.stack(outs, axis=0)
```

```python
import functools

import numpy as np
import jax
import jax.numpy as jnp
from jax import lax
from jax.experimental import pallas as pl
from jax.experimental.pallas import tpu as pltpu

D_MODEL = 2048
HEAD_DIM = 128
N_ATT_HEADS = 8
N_KV_GROUPS = 2
HEADS_PER_GROUP = N_ATT_HEADS // N_KV_GROUPS
ATT_WIDTH = N_ATT_HEADS * HEAD_DIM
KV_WIDTH = N_KV_GROUPS * HEAD_DIM
CONV_WIDTH = 1024
MIX_WIDTH = ATT_WIDTH + CONV_WIDTH
N_MIX_GROUPS = MIX_WIDTH // HEAD_DIM
N_BRANCH = 3
CMP_BLOCK = 32
CMP_STRIDE = 16
CMP_HIDDEN = 512
SLC_BLOCK = 64
N_SELECT = 16
WINDOW = 512
Q_BLOCK = 128
CONV_K = 3
PEER_HEADS = 8
N_KEYS = 128
PEER_HALF = 128
PEER_TOPK = 16
PEER_K = PEER_HEADS * PEER_TOPK
DEPTH = 1
ALPHA = (2.0 * DEPTH) ** 0.25
LN_EPS = 1e-5
RMS_EPS = 1e-6
NEG_INF = -1e30
FORCE_BONUS = 1e4

LANES = 128
SUBLANES = 8
VMEM_LIMIT_BYTES = 56 * 1024 * 1024

GATE_PAD = LANES
OFF_Q = 0
OFF_U = OFF_Q + ATT_WIDTH
OFF_GB = OFF_U + CONV_WIDTH
OFF_GC = OFF_GB + CONV_WIDTH
OFF_KC = OFF_GC + CONV_WIDTH
OFF_VC = OFF_KC + KV_WIDTH
OFF_KS = OFF_VC + KV_WIDTH
OFF_VS = OFF_KS + KV_WIDTH
OFF_KW = OFF_VS + KV_WIDTH
OFF_VW = OFF_KW + KV_WIDTH
OFF_GATE = OFF_VW + KV_WIDTH
PROJ_W = OFF_GATE + GATE_PAD

SLC_TILE = 512
SLC_PER_TILE = SLC_TILE // SLC_BLOCK
WIN_KEYS = WINDOW + Q_BLOCK
WIN_BLOCKS = WIN_KEYS // Q_BLOCK

PEER_TB = 8
PEER_ROWS = PEER_TB * PEER_K
D_CHUNKS = D_MODEL // LANES
ISSUE_UNROLL = 32
assert D_CHUNKS & (D_CHUNKS - 1) == 0

_BF16 = jnp.bfloat16
_F32 = jnp.float32
_NT = (((1,), (1,)), ((), ()))


def _cparams(sem):
    return pltpu.CompilerParams(dimension_semantics=sem, vmem_limit_bytes=VMEM_LIMIT_BYTES)


def _matmul_kernel(x_ref, w_ref, o_ref):
    o_ref[...] = jnp.dot(x_ref[...], w_ref[...], preferred_element_type=_F32)


def in_proj(xb, wb):
    S, K = xb.shape
    N = wb.shape[1]
    tm = min(512, S)
    tn = N // 5
    return pl.pallas_call(
        _matmul_kernel,
        grid=(N // tn, S // tm),
        in_specs=[pl.BlockSpec((tm, K), lambda j, i: (i, 0)),
                  pl.BlockSpec((K, tn), lambda j, i: (0, j))],
        out_specs=pl.BlockSpec((tm, tn), lambda j, i: (i, j)),
        out_shape=jax.ShapeDtypeStruct((S, N), _F32),
        compiler_params=_cparams(("arbitrary", "arbitrary")),
        name="in_proj",
    )(xb, wb)


def _compress_kernel(hb_ref, pos_ref, w1_ref, b1_ref, w2_ref, o_ref):
    nh = hb_ref.shape[1]
    half = hb_ref.shape[2]
    hb = hb_ref[0]
    top = (hb + pos_ref[:, :half]).astype(_BF16)
    bot = (hb + pos_ref[:, half:]).astype(_BF16)
    a = jnp.dot(top, w1_ref[:half, :], preferred_element_type=_F32)
    b = jnp.dot(bot, w1_ref[half:, :], preferred_element_type=_F32)
    hidden = a + pltpu.roll(b, nh - 1, 0) + b1_ref[...]
    act = jax.nn.gelu(hidden).astype(_BF16)
    o_ref[0] = jnp.dot(act, w2_ref[...], preferred_element_type=_F32)


def compress(hb, pos, w1, b1, w2):
    G, nh, half = hb.shape
    posflat = pos.reshape(1, CMP_BLOCK * HEAD_DIM)
    return pl.pallas_call(
        _compress_kernel,
        grid=(G,),
        in_specs=[pl.BlockSpec((1, nh, half), lambda g: (g, 0, 0)),
                  pl.BlockSpec((1, 2 * half), lambda g: (0, 0)),
                  pl.BlockSpec((2 * half, CMP_HIDDEN), lambda g: (0, 0)),
                  pl.BlockSpec((1, CMP_HIDDEN), lambda g: (0, 0)),
                  pl.BlockSpec((CMP_HIDDEN, HEAD_DIM), lambda g: (0, 0))],
        out_specs=pl.BlockSpec((1, nh, HEAD_DIM), lambda g: (g, 0, 0)),
        out_shape=jax.ShapeDtypeStruct((G, nh, HEAD_DIM), _F32),
        compiler_params=_cparams(("arbitrary",)),
        name="compress",
    )(hb, posflat, w1.astype(_BF16), b1.reshape(1, CMP_HIDDEN), w2.astype(_BF16))


def _nsa_kernel(q_ref, kc_ref, vct_ref, ks_ref, vst_ref, kw_ref, vwt_ref, gl_ref, slope_ref, ovt_ref,
                o_ref, sel_ref, m_ref, l_ref, acc_ref, *, n_sel):
    nb = pl.program_id(1)
    s0 = nb * Q_BLOCK
    R = HEADS_PER_GROUP
    L = R * Q_BLOCK
    nh = kc_ref.shape[1]
    n_slc = ovt_ref.shape[0]
    scale = HEAD_DIM ** -0.5

    qs = jnp.concatenate([q_ref[:, r * HEAD_DIM:(r + 1) * HEAD_DIM] for r in range(R)], axis=0)
    qs = (qs * scale).astype(_BF16)
    slope = slope_ref[0]
    lane = lax.broadcasted_iota(jnp.int32, (1, L), 1)
    t_i = s0 + (lane & (Q_BLOCK - 1))
    t_f = t_i.astype(_F32)

    sc = lax.dot_general(kc_ref[0], qs, _NT, preferred_element_type=_F32)
    n_io = lax.broadcasted_iota(jnp.int32, (nh, L), 0)
    valid = (n_io * CMP_STRIDE + (CMP_BLOCK - 1)) <= t_i
    cpos = n_io.astype(_F32) * CMP_STRIDE + (CMP_BLOCK - 1) / 2.0
    s = jnp.where(valid, sc - slope * (t_f - cpos), NEG_INF)
    m = jnp.max(s, axis=0, keepdims=True)
    e = jnp.where(valid, jnp.exp(s - m), 0.0)
    l = jnp.sum(e, axis=0, keepdims=True)
    p = e * (1.0 / jnp.maximum(l, 1e-30))
    o_cmp = jnp.dot(vct_ref[0], p.astype(_BF16), preferred_element_type=_F32)

    ps = p[:, 0:Q_BLOCK]
    for r in range(1, R):
        ps = ps + p[:, r * Q_BLOCK:(r + 1) * Q_BLOCK]
    hi = ps.astype(_BF16)
    r1 = ps - hi.astype(_F32)
    mid = r1.astype(_BF16)
    lo = (r1 - mid.astype(_F32)).astype(_BF16)
    ovt = ovt_ref[...]
    imp = (jnp.dot(ovt, hi, preferred_element_type=_F32) + jnp.dot(ovt, mid, preferred_element_type=_F32)
           + jnp.dot(ovt, lo, preferred_element_type=_F32))

    j_io = lax.broadcasted_iota(jnp.int32, (n_slc, Q_BLOCK), 0)
    tq = s0 + lax.broadcasted_iota(jnp.int32, (1, Q_BLOCK), 1)
    cur = tq >> 6
    valid_s = (j_io * SLC_BLOCK) <= tq
    forced = (j_io == 0) | (j_io == cur) | (j_io == cur - 1)
    score = jnp.where(valid_s, imp + jnp.where(forced, FORCE_BONUS, 0.0), -1.0)

    def pick(_, carry):
        work, sel = carry
        mx = jnp.max(work, axis=0, keepdims=True)
        first = jnp.min(jnp.where(work == mx, j_io, n_slc), axis=0, keepdims=True)
        hit = j_io == first
        return jnp.where(hit, -jnp.inf, work), jnp.where(hit, 1.0, sel)

    _, sel = lax.fori_loop(0, n_sel, pick, (score, jnp.zeros((n_slc, Q_BLOCK), _F32)))
    sel_ref[...] = sel

    m_ref[...] = jnp.full((1, L), NEG_INF, _F32)
    l_ref[...] = jnp.zeros((1, L), _F32)
    acc_ref[...] = jnp.zeros((HEAD_DIM, L), _F32)
    key_io = lax.broadcasted_iota(jnp.int32, (SLC_TILE, L), 0)

    def slc_tile(kt, carry):
        selt = sel_ref[pl.ds(pl.multiple_of(kt * SLC_PER_TILE, SLC_PER_TILE), SLC_PER_TILE), :]

        @pl.when(jnp.max(selt) > 0.0)
        def _():
            sx = jnp.broadcast_to(selt[:, None, :], (SLC_PER_TILE, SLC_BLOCK, Q_BLOCK)).reshape(SLC_TILE, Q_BLOCK)
            sx = jnp.concatenate([sx] * R, axis=1)
            dist = t_i - (kt * SLC_TILE + key_io)
            ok = (sx > 0.0) & (dist >= 0)
            sr = lax.dot_general(ks_ref[0, kt], qs, _NT, preferred_element_type=_F32)
            st = jnp.where(ok, sr - slope * dist.astype(_F32), NEG_INF)
            m_old = m_ref[...]
            m_new = jnp.maximum(m_old, jnp.max(st, axis=0, keepdims=True))
            alpha = jnp.exp(m_old - m_new)
            et = jnp.where(ok, jnp.exp(st - m_new), 0.0)
            l_ref[...] = alpha * l_ref[...] + jnp.sum(et, axis=0, keepdims=True)
            acc_ref[...] = alpha * acc_ref[...] + jnp.dot(vst_ref[0, kt], et.astype(_BF16),
                                                          preferred_element_type=_F32)
            m_ref[...] = m_new
        return carry

    lax.fori_loop(0, (s0 + Q_BLOCK + SLC_TILE - 1) // SLC_TILE, slc_tile, 0)
    o_slc = acc_ref[...] * (1.0 / l_ref[...])

    start = pl.multiple_of(jnp.maximum(s0 - WINDOW, 0), Q_BLOCK)
    sw = lax.dot_general(kw_ref[0, pl.ds(start, WIN_KEYS), :], qs, _NT, preferred_element_type=_F32)
    dw = t_i - (start + lax.broadcasted_iota(jnp.int32, (WIN_KEYS, L), 0))
    okw = (dw >= 0) & (dw < WINDOW)
    s = jnp.where(okw, sw - slope * dw.astype(_F32), NEG_INF)
    m = jnp.max(s, axis=0, keepdims=True)
    e = jnp.where(okw, jnp.exp(s - m), 0.0)
    pw = (e * (1.0 / jnp.sum(e, axis=0, keepdims=True))).astype(_BF16)
    b0 = start // Q_BLOCK
    o_win = jnp.dot(vwt_ref[0, b0], pw[0:Q_BLOCK], preferred_element_type=_F32)
    for i in range(1, WIN_BLOCKS):
        o_win = o_win + jnp.dot(vwt_ref[0, b0 + i], pw[i * Q_BLOCK:(i + 1) * Q_BLOCK],
                                preferred_element_type=_F32)

    sg = jax.nn.sigmoid(gl_ref[0, 0])
    o = sg[0:1] * o_cmp + sg[1:2] * o_slc + sg[2:3] * o_win
    for r in range(R):
        o_ref[:, r * HEAD_DIM:(r + 1) * HEAD_DIM] = o[:, r * Q_BLOCK:(r + 1) * Q_BLOCK].T


def nsa(proj, kc, vc):
    S = proj.shape[0]
    G, R = N_KV_GROUPS, HEADS_PER_GROUP
    nh = S // CMP_STRIDE
    n_slc = S // SLC_BLOCK
    n_qb = S // Q_BLOCK
    n_kt = S // SLC_TILE
    n_sel = min(N_SELECT, n_slc)
    L = R * Q_BLOCK

    def grp(off):
        return proj[:, off:off + KV_WIDTH].reshape(S, G, HEAD_DIM).transpose(1, 0, 2)

    kcb = kc.astype(_BF16)
    vct = vc.transpose(0, 2, 1).astype(_BF16)
    ks3 = grp(OFF_KS).astype(_BF16).reshape(G, n_kt, SLC_TILE, HEAD_DIM)
    vst3 = grp(OFF_VS).astype(_BF16).reshape(G, n_kt, SLC_TILE, HEAD_DIM).transpose(0, 1, 3, 2)
    kw = grp(OFF_KW).astype(_BF16)
    vwt3 = grp(OFF_VW).astype(_BF16).reshape(G, n_qb, Q_BLOCK, HEAD_DIM).transpose(0, 1, 3, 2)
    gl = proj[:, OFF_GATE:OFF_GATE + N_ATT_HEADS * N_BRANCH].reshape(n_qb, Q_BLOCK, G, R, N_BRANCH)
    gl = gl.transpose(2, 0, 4, 3, 1).reshape(G, n_qb, N_BRANCH, L)
    head = np.arange(N_ATT_HEADS, dtype=np.float64).reshape(G, R)
    slopes = np.repeat(2.0 ** (-8.0 * (head + 1) / N_ATT_HEADS), Q_BLOCK, axis=1).reshape(G, 1, L)
    ci = np.arange(nh)[None, :] * CMP_STRIDE
    sj = np.arange(n_slc)[:, None] * SLC_BLOCK
    ovt = ((ci < sj + SLC_BLOCK) & (ci + CMP_BLOCK > sj) & (np.arange(nh)[None, :] < nh - 1))

    return pl.pallas_call(
        functools.partial(_nsa_kernel, n_sel=n_sel),
        grid=(G, n_qb),
        in_specs=[
            pl.BlockSpec((Q_BLOCK, L), lambda g, i: (i, g)),
            pl.BlockSpec((1, nh, HEAD_DIM), lambda g, i: (g, 0, 0)),
            pl.BlockSpec((1, HEAD_DIM, nh), lambda g, i: (g, 0, 0)),
            pl.BlockSpec((1, n_kt, SLC_TILE, HEAD_DIM), lambda g, i: (g, 0, 0, 0)),
            pl.BlockSpec((1, n_kt, HEAD_DIM, SLC_TILE), lambda g, i: (g, 0, 0, 0)),
            pl.BlockSpec((1, S, HEAD_DIM), lambda g, i: (g, 0, 0)),
            pl.BlockSpec((1, n_qb, HEAD_DIM, Q_BLOCK), lambda g, i: (g, 0, 0, 0)),
            pl.BlockSpec((1, 1, N_BRANCH, L), lambda g, i: (g, i, 0, 0)),
            pl.BlockSpec((1, 1, L), lambda g, i: (g, 0, 0)),
            pl.BlockSpec((n_slc, nh), lambda g, i: (0, 0)),
        ],
        out_specs=pl.BlockSpec((Q_BLOCK, L), lambda g, i: (i, g)),
        out_shape=jax.ShapeDtypeStruct((S, ATT_WIDTH), _F32),
        scratch_shapes=[pltpu.VMEM((n_slc, Q_BLOCK), _F32),
                        pltpu.VMEM((1, L), _F32),
                        pltpu.VMEM((1, L), _F32),
                        pltpu.VMEM((HEAD_DIM, L), _F32)],
        compiler_params=_cparams(("arbitrary", "arbitrary")),
        name="nsa",
    )(proj, kcb, vct, ks3, vst3, kw, vwt3, gl, jnp.asarray(slopes, _F32), jnp.asarray(ovt, _BF16))


def _layernorm_rows(r, g, b):
    mu = jnp.mean(r, axis=-1, keepdims=True)
    c = r - mu
    var = jnp.mean(c * c, axis=-1, keepdims=True)
    return c * lax.rsqrt(var + LN_EPS) * g + b


def _mix_out_kernel(x_ref, att_ref, u_ref, gb_ref, gc_ref, uh_ref, gch_ref, cw_ref, hg_ref, wo_ref,
                    g1_ref, b1_ref, h_ref):
    i = pl.program_id(0)
    tm = x_ref.shape[0]
    z = gc_ref[...] * u_ref[...]
    zh = jnp.where(i > 0, gch_ref[...] * uh_ref[...], 0.0)
    zz = jnp.concatenate([zh, z], axis=0)
    z1 = pltpu.roll(zz, 1, 0)[SUBLANES:]
    z2 = pltpu.roll(zz, 2, 0)[SUBLANES:]
    y = cw_ref[0:1, :] * z2 + cw_ref[1:2, :] * z1 + cw_ref[2:3, :] * z
    o_conv = gb_ref[...] * y
    mix = jnp.concatenate([att_ref[...], o_conv], axis=1)
    parts = []
    for k in range(N_MIX_GROUPS):
        blk = mix[:, k * HEAD_DIM:(k + 1) * HEAD_DIM]
        ms = jnp.mean(blk * blk, axis=-1, keepdims=True)
        parts.append(blk * lax.rsqrt(ms + RMS_EPS))
    mixn = (jnp.concatenate(parts, axis=1) * hg_ref[...]).astype(_BF16)
    r = ALPHA * x_ref[...] + jnp.dot(mixn, wo_ref[...], preferred_element_type=_F32)
    h_ref[...] = _layernorm_rows(r, g1_ref[...], b1_ref[...])


def mix_out(x2, o_att, proj, conv_w, head_norm_g, w_out, ln_g, ln_b):
    S = x2.shape[0]
    tm = min(256, S)
    hb = tm // SUBLANES
    cblk = lambda off: off // CONV_WIDTH
    halo = lambda off: pl.BlockSpec((SUBLANES, CONV_WIDTH), lambda i: (jnp.maximum(i * hb - 1, 0), cblk(off)))
    tile = lambda off: pl.BlockSpec((tm, CONV_WIDTH), lambda i: (i, cblk(off)))
    full = lambda shape: pl.BlockSpec(shape, lambda i: (0, 0))
    cw = jnp.zeros((SUBLANES, CONV_WIDTH), _F32).at[:CONV_K].set(conv_w)
    return pl.pallas_call(
        _mix_out_kernel,
        grid=(S // tm,),
        in_specs=[pl.BlockSpec((tm, D_MODEL), lambda i: (i, 0)),
                  pl.BlockSpec((tm, ATT_WIDTH), lambda i: (i, 0)),
                  tile(OFF_U), tile(OFF_GB), tile(OFF_GC), halo(OFF_U), halo(OFF_GC),
                  full((SUBLANES, CONV_WIDTH)), full((1, MIX_WIDTH)), full((MIX_WIDTH, D_MODEL)),
                  full((1, D_MODEL)), full((1, D_MODEL))],
        out_specs=pl.BlockSpec((tm, D_MODEL), lambda i: (i, 0)),
        out_shape=jax.ShapeDtypeStruct((S, D_MODEL), _F32),
        compiler_params=_cparams(("arbitrary",)),
        name="mix_out",
    )(x2, o_att, proj, proj, proj, proj, proj, cw, head_norm_g.reshape(1, MIX_WIDTH), w_out.astype(_BF16),
      ln_g.reshape(1, D_MODEL), ln_b.reshape(1, D_MODEL))


def _topk_rows(x, k):
    R, L = x.shape
    io = lax.broadcasted_iota(jnp.int32, (R, L), 0)
    ko = lax.broadcasted_iota(jnp.int32, (k, L), 0)
    vals = jnp.zeros((k, L), _F32)
    idxs = jnp.zeros((k, L), jnp.int32)
    for i in range(k):
        mx = jnp.max(x, axis=0, keepdims=True)
        first = jnp.min(jnp.where(x == mx, io, R), axis=0, keepdims=True)
        x = jnp.where(io == first, -jnp.inf, x)
        vals = jnp.where(ko == i, mx, vals)
        idxs = jnp.where(ko == i, first, idxs)
    return vals, idxs


def _take_rows(table, idx):
    out = jnp.zeros(idx.shape, table.dtype)
    for a in range(table.shape[0]):
        out = jnp.where(idx == a, table[a:a + 1, :], out)
    return out


def _peer_route_kernel(h_ref, wq_ref, sk_ref, eid_ref, g_ref):
    qp = jnp.dot(h_ref[...].astype(_BF16), wq_ref[...], preferred_element_type=_F32)
    for hd in range(PEER_HEADS):
        tops = []
        for c in range(2):
            j = hd * 2 + c
            qhc = qp[:, j * PEER_HALF:(j + 1) * PEER_HALF].astype(_BF16)
            st = lax.dot_general(sk_ref[j], qhc, _NT, preferred_element_type=_F32)
            tops.append(_topk_rows(st, PEER_TOPK))
        (v1, i1), (v2, i2) = tops
        nb = [PEER_TOPK // (a + 1) for a in range(PEER_TOPK)]
        tt = i1.shape[1]
        pad = -sum(nb) % SUBLANES
        cand = jnp.concatenate([v1[a:a + 1, :] + v2[0:nb[a], :] for a in range(PEER_TOPK)]
                               + [jnp.full((pad, tt), -jnp.inf, _F32)], axis=0)
        bs, bi = _topk_rows(cand, PEER_TOPK)
        zpad = [jnp.zeros((pad, tt), jnp.int32)]
        t1 = jnp.concatenate([jnp.broadcast_to(i1[a:a + 1, :], (nb[a], tt)) for a in range(PEER_TOPK)] + zpad,
                             axis=0)
        t2 = jnp.concatenate([i2[0:nb[a], :] for a in range(PEER_TOPK)] + zpad, axis=0)
        e1 = _take_rows(t1, bi)
        e2 = _take_rows(t2, bi)
        ex = jnp.exp(bs - jnp.max(bs, axis=0, keepdims=True))
        eid_ref[hd * PEER_TOPK:(hd + 1) * PEER_TOPK, :] = e1 * N_KEYS + e2
        g_ref[hd * PEER_TOPK:(hd + 1) * PEER_TOPK, :] = ex * (1.0 / jnp.sum(ex, axis=0, keepdims=True))


def peer_route(h, w_query, sub_keys):
    S = h.shape[0]
    tt = LANES
    skb = sub_keys.reshape(PEER_HEADS * 2, N_KEYS, PEER_HALF).astype(_BF16)
    return pl.pallas_call(
        _peer_route_kernel,
        grid=(S // tt,),
        in_specs=[pl.BlockSpec((tt, D_MODEL), lambda i: (i, 0)),
                  pl.BlockSpec((D_MODEL, PEER_HEADS * 2 * PEER_HALF), lambda i: (0, 0)),
                  pl.BlockSpec((PEER_HEADS * 2, N_KEYS, PEER_HALF), lambda i: (0, 0, 0))],
        out_specs=[pl.BlockSpec((PEER_K, tt), lambda i: (0, i)),
                   pl.BlockSpec((PEER_K, tt), lambda i: (0, i))],
        out_shape=[jax.ShapeDtypeStruct((PEER_K, S), jnp.int32),
                   jax.ShapeDtypeStruct((PEER_K, S), _F32)],
        compiler_params=_cparams(("arbitrary",)),
        name="peer_route",
    )(h, w_query.astype(_BF16), skb)


def _row_copy(uv_hbm, buf, sem, e, r):
    return pltpu.make_async_copy(uv_hbm.at[e], buf.at[r], sem)


def _wait_all_rows(buf, sem):
    pltpu.make_async_copy(buf, buf, sem).wait()


def _sum_lane_groups(x):
    n = x.shape[1]
    lane = lax.broadcasted_iota(jnp.int32, x.shape, 1)
    sh = 1
    while sh < D_CHUNKS:
        lower = pltpu.roll(x, sh, 1)
        upper = pltpu.roll(x, n - sh, 1)
        x = x + jnp.where((lane & sh) == 0, upper, lower)
        sh *= 2
    return x


def _peer_batch(src, h3, gx, g2, b2):
    wide = PEER_K * D_CHUNKS
    diag = (lax.broadcasted_iota(jnp.int32, (D_CHUNKS, wide), 1) & (D_CHUNKS - 1)) == \
        lax.broadcasted_iota(jnp.int32, (D_CHUNKS, wide), 0)
    parts = []
    for t in range(PEER_TB):
        ut = src[t * PEER_K:(t + 1) * PEER_K, 0:D_CHUNKS, :].reshape(wide, LANES)
        y = lax.dot_general(h3[t].astype(_BF16), ut, _NT, preferred_element_type=_F32)
        parts.append(jnp.sum(jnp.where(diag, y, 0.0), axis=0, keepdims=True))
    a = _sum_lane_groups(jnp.concatenate(parts, axis=0))
    c = jax.nn.gelu(a) * gx
    outs = []
    for t in range(PEER_TB):
        lt = jnp.where(diag, c[t:t + 1, :], 0.0).astype(_BF16)
        vt = src[t * PEER_K:(t + 1) * PEER_K, D_CHUNKS:2 * D_CHUNKS, :].reshape(wide, LANES)
        r = ALPHA * h3[t] + jnp.dot(lt, vt, preferred_element_type=_F32)
        mu = jnp.mean(r, axis=(0, 1), keepdims=True)
        d = r - mu
        var = jnp.mean(d * d, axis=(0, 1), keepdims=True)
        outs.append(d * lax.rsqrt(var + LN_EPS) * g2 + b2)
    return jnp.stack(outs, axis=0)


def _peer_apply_kernel(eid_cur_ref, eid_nxt_ref, h_ref, g_ref, g2_ref, b2_ref, uv_hbm, o_ref,
                       buf_a, buf_b, sem):
    i = pl.program_id(0)
    n = pl.num_programs(0)

    @pl.when(i == 0)
    def _():
        def body(c, carry):
            for j in range(ISSUE_UNROLL):
                r = c * ISSUE_UNROLL + j
                _row_copy(uv_hbm, buf_a, sem.at[0], eid_cur_ref[0, 0, r], r).start()
            return carry
        lax.fori_loop(0, PEER_ROWS // ISSUE_UNROLL, body, 0)

    _wait_all_rows(buf_a, sem.at[0])
    for r in range(PEER_ROWS):
        _row_copy(uv_hbm, buf_b, sem.at[1], eid_cur_ref[0, 0, PEER_ROWS + r], r).start(priority=r % 2)
    o_ref[0:PEER_TB] = _peer_batch(buf_a, h_ref[0:PEER_TB], g_ref[0:PEER_TB, :], g2_ref[...], b2_ref[...])

    _wait_all_rows(buf_b, sem.at[1])
    for r in range(PEER_ROWS):
        _row_copy(uv_hbm, buf_a, sem.at[0], eid_nxt_ref[0, 0, r], r).start(priority=r % 2)
    o_ref[PEER_TB:2 * PEER_TB] = _peer_batch(buf_b, h_ref[PEER_TB:2 * PEER_TB], g_ref[PEER_TB:2 * PEER_TB, :],
                                             g2_ref[...], b2_ref[...])

    @pl.when(i == n - 1)
    def _():
        _wait_all_rows(buf_a, sem.at[0])


def pack_expert_rows(expert_u, expert_v):
    E = expert_u.shape[0]
    tile = lambda w: w.astype(_BF16).reshape(E, D_CHUNKS, LANES)
    return jnp.concatenate([tile(expert_u), tile(expert_v)], axis=1)


def peer_apply(h, eid, gates, uvp, ln_g, ln_b):
    S = h.shape[0]
    npair = S // (2 * PEER_TB)
    eid3 = eid.reshape(npair, 1, 2 * PEER_ROWS)
    wide = PEER_K * D_CHUNKS
    gx = jnp.repeat(gates, D_CHUNKS, axis=1)
    tile3 = lambda a: a.reshape(-1, D_CHUNKS, LANES)
    gbuf = pltpu.VMEM((PEER_ROWS, 2 * D_CHUNKS, LANES), uvp.dtype)
    out = pl.pallas_call(
        _peer_apply_kernel,
        grid=(npair,),
        in_specs=[
            pl.BlockSpec((1, 1, 2 * PEER_ROWS), lambda i: (i, 0, 0), memory_space=pltpu.SMEM),
            pl.BlockSpec((1, 1, 2 * PEER_ROWS), lambda i: (jnp.minimum(i + 1, npair - 1), 0, 0),
                         memory_space=pltpu.SMEM),
            pl.BlockSpec((2 * PEER_TB, D_CHUNKS, LANES), lambda i: (i, 0, 0)),
            pl.BlockSpec((2 * PEER_TB, wide), lambda i: (i, 0)),
            pl.BlockSpec((D_CHUNKS, LANES), lambda i: (0, 0)),
            pl.BlockSpec((D_CHUNKS, LANES), lambda i: (0, 0)),
            pl.BlockSpec(memory_space=pl.ANY),
        ],
        out_specs=pl.BlockSpec((2 * PEER_TB, D_CHUNKS, LANES), lambda i: (i, 0, 0)),
        out_shape=jax.ShapeDtypeStruct((S, D_CHUNKS, LANES), _F32),
        scratch_shapes=[gbuf, gbuf, pltpu.SemaphoreType.DMA((2,))],
        compiler_params=_cparams(("arbitrary",)),
        name="peer_apply",
    )(eid3, eid3, tile3(h), gx, ln_g.reshape(D_CHUNKS, LANES), ln_b.reshape(D_CHUNKS, LANES), uvp)
    return out.reshape(S, D_MODEL)


def _pad_w_in(w):
    offs = np.cumsum([0, ATT_WIDTH] + [KV_WIDTH] * 6 + [N_ATT_HEADS * N_BRANCH] + [CONV_WIDTH] * 3)
    seg = lambda k: w[:, offs[k]:offs[k + 1]]
    q, kc, vc, ks, vs, kw, vw, gate, u, gb, gc = [seg(k) for k in range(11)]
    gate = jnp.pad(gate, ((0, 0), (0, GATE_PAD - gate.shape[1])))
    return jnp.concatenate([q, u, gb, gc, kc, vc, ks, vs, kw, vw, gate], axis=1)


def _layer(x2, w_in, cmp_k, cmp_v, conv_w, head_norm_g, w_out, ln1, w_query, sub_keys, expert_u, expert_v, ln2):
    S = x2.shape[0]
    G = N_KV_GROUPS
    proj = in_proj(x2.astype(_BF16), _pad_w_in(w_in).astype(_BF16))

    def half_blocks(off):
        t = proj[:, off:off + KV_WIDTH].reshape(S, G, HEAD_DIM).transpose(1, 0, 2)
        return t.reshape(G, S // CMP_STRIDE, CMP_STRIDE * HEAD_DIM)

    kc = compress(half_blocks(OFF_KC), *cmp_k)
    vc = compress(half_blocks(OFF_VC), *cmp_v)
    o_att = nsa(proj, kc, vc)
    h = mix_out(x2, o_att, proj, conv_w, head_norm_g, w_out, *ln1)
    eid_t, g_t = peer_route(h, w_query, sub_keys)
    return peer_apply(h, eid_t.T, g_t.T, pack_expert_rows(expert_u, expert_v), *ln2)


def kernel(x, w_in, cmp_k_pos, cmp_k_w1, cmp_k_b1, cmp_k_w2, cmp_v_pos, cmp_v_w1, cmp_v_b1, cmp_v_w2, conv_w, head_norm_g, w_out, ln1_g, ln1_b, w_query, sub_keys, expert_u, expert_v, ln2_g, ln2_b):
    B, S, D = x.shape
    outs = []
    for b in range(B):
        xb = x[b]
        for l in range(w_in.shape[0]):
            xb = _layer(xb, w_in[l],
                        (cmp_k_pos[l], cmp_k_w1[l], cmp_k_b1[l], cmp_k_w2[l]),
                        (cmp_v_pos[l], cmp_v_w1[l], cmp_v_b1[l], cmp_v_w2[l]),
                        conv_w[l], head_norm_g[l], w_out[l], (ln1_g[l], ln1_b[l]),
                        w_query[l], sub_keys[l], expert_u[l], expert_v[l], (ln2_g[l], ln2_b[l]))
        outs.append(xb)
    return jnp.stack(outs, axis=0)
```

```python
import functools

import numpy as np
import jax
import jax.numpy as jnp
from jax import lax
from jax.experimental import pallas as pl
from jax.experimental.pallas import tpu as pltpu

D_MODEL = 2048
HEAD_DIM = 128
N_ATT_HEADS = 8
N_KV_GROUPS = 2
HEADS_PER_GROUP = N_ATT_HEADS // N_KV_GROUPS
ATT_WIDTH = N_ATT_HEADS * HEAD_DIM
KV_WIDTH = N_KV_GROUPS * HEAD_DIM
CONV_WIDTH = 1024
MIX_WIDTH = ATT_WIDTH + CONV_WIDTH
N_MIX_GROUPS = MIX_WIDTH // HEAD_DIM
N_BRANCH = 3
CMP_BLOCK = 32
CMP_STRIDE = 16
CMP_HIDDEN = 512
SLC_BLOCK = 64
N_SELECT = 16
WINDOW = 512
Q_BLOCK = 128
CONV_K = 3
PEER_HEADS = 8
N_KEYS = 128
PEER_HALF = 128
PEER_TOPK = 16
PEER_K = PEER_HEADS * PEER_TOPK
DEPTH = 1
ALPHA = (2.0 * DEPTH) ** 0.25
LN_EPS = 1e-5
RMS_EPS = 1e-6
NEG_INF = -1e30
FORCE_BONUS = 1e4

LANES = 128
SUBLANES = 8
VMEM_LIMIT_BYTES = 56 * 1024 * 1024

GATE_PAD = LANES
OFF_Q = 0
OFF_U = OFF_Q + ATT_WIDTH
OFF_GB = OFF_U + CONV_WIDTH
OFF_GC = OFF_GB + CONV_WIDTH
OFF_KC = OFF_GC + CONV_WIDTH
OFF_VC = OFF_KC + KV_WIDTH
OFF_KS = OFF_VC + KV_WIDTH
OFF_VS = OFF_KS + KV_WIDTH
OFF_KW = OFF_VS + KV_WIDTH
OFF_VW = OFF_KW + KV_WIDTH
OFF_GATE = OFF_VW + KV_WIDTH
PROJ_W = OFF_GATE + GATE_PAD

SLC_TILE = 512
SLC_PER_TILE = SLC_TILE // SLC_BLOCK
SLC_BLOCK_LOG2 = SLC_BLOCK.bit_length() - 1
CMP_STRIDE_LOG2 = CMP_STRIDE.bit_length() - 1
assert 1 << SLC_BLOCK_LOG2 == SLC_BLOCK and 1 << CMP_STRIDE_LOG2 == CMP_STRIDE
WIN_KEYS = WINDOW + Q_BLOCK
WIN_BLOCKS = WIN_KEYS // Q_BLOCK

PEER_TB = 8
PEER_ROWS = PEER_TB * PEER_K
D_CHUNKS = D_MODEL // LANES
ISSUE_UNROLL = 32
assert D_CHUNKS & (D_CHUNKS - 1) == 0

_BF16 = jnp.bfloat16
_F32 = jnp.float32
_NT = (((1,), (1,)), ((), ()))


def _cparams(sem):
    return pltpu.CompilerParams(dimension_semantics=sem, vmem_limit_bytes=VMEM_LIMIT_BYTES)


def _matmul_kernel(x_ref, w_ref, o_ref, ob_ref):
    acc = jnp.dot(x_ref[...], w_ref[...], preferred_element_type=_F32)
    o_ref[...] = acc
    ob_ref[...] = acc.astype(ob_ref.dtype)


def in_proj(xb, wb):
    S, K = xb.shape
    N = wb.shape[1]
    tm = min(512, S)
    tn = N // 5
    return pl.pallas_call(
        _matmul_kernel,
        grid=(N // tn, S // tm),
        in_specs=[pl.BlockSpec((tm, K), lambda j, i: (i, 0)),
                  pl.BlockSpec((K, tn), lambda j, i: (0, j))],
        out_specs=[pl.BlockSpec((tm, tn), lambda j, i: (i, j)),
                   pl.BlockSpec((tm, tn), lambda j, i: (i, j))],
        out_shape=[jax.ShapeDtypeStruct((S, N), _F32), jax.ShapeDtypeStruct((S, N), wb.dtype)],
        compiler_params=_cparams(("arbitrary", "arbitrary")),
        name="in_proj",
    )(xb, wb)


def _compress_kernel(hb_ref, pos_ref, w1_ref, b1_ref, w2_ref, o_ref):
    nh = hb_ref.shape[1]
    half = hb_ref.shape[2]
    hb = hb_ref[0]
    top = (hb + pos_ref[:, :half]).astype(_BF16)
    bot = (hb + pos_ref[:, half:]).astype(_BF16)
    a = jnp.dot(top, w1_ref[:half, :], preferred_element_type=_F32)
    b = jnp.dot(bot, w1_ref[half:, :], preferred_element_type=_F32)
    hidden = a + pltpu.roll(b, nh - 1, 0) + b1_ref[...]
    act = jax.nn.gelu(hidden).astype(_BF16)
    o_ref[0] = jnp.dot(act, w2_ref[...], preferred_element_type=_F32)


def compress(hb, pos, w1, b1, w2):
    G, nh, half = hb.shape
    posflat = pos.reshape(1, CMP_BLOCK * HEAD_DIM)
    return pl.pallas_call(
        _compress_kernel,
        grid=(G,),
        in_specs=[pl.BlockSpec((1, nh, half), lambda g: (g, 0, 0)),
                  pl.BlockSpec((1, 2 * half), lambda g: (0, 0)),
                  pl.BlockSpec((2 * half, CMP_HIDDEN), lambda g: (0, 0)),
                  pl.BlockSpec((1, CMP_HIDDEN), lambda g: (0, 0)),
                  pl.BlockSpec((CMP_HIDDEN, HEAD_DIM), lambda g: (0, 0))],
        out_specs=pl.BlockSpec((1, nh, HEAD_DIM), lambda g: (g, 0, 0)),
        out_shape=jax.ShapeDtypeStruct((G, nh, HEAD_DIM), _F32),
        compiler_params=_cparams(("arbitrary",)),
        name="compress",
    )(hb, posflat, w1.astype(_BF16), b1.reshape(1, CMP_HIDDEN), w2.astype(_BF16))


def _nsa_kernel(q_ref, kc_ref, vct_ref, ks_ref, vst_ref, kw_ref, vwt_ref, gl_ref, slope_ref, cb_ref, rb_ref,
                ovt_ref, o_ref, sel_ref, m_ref, l_ref, acc_ref, flag_ref, *, n_sel):
    nb = pl.program_id(1)
    s0 = nb * Q_BLOCK
    R = HEADS_PER_GROUP
    L = R * Q_BLOCK
    nh = kc_ref.shape[1]
    n_slc = ovt_ref.shape[0]
    scale = HEAD_DIM ** -0.5

    qs = jnp.concatenate([q_ref[:, r * HEAD_DIM:(r + 1) * HEAD_DIM] for r in range(R)], axis=0)
    qs = (qs * scale).astype(_BF16)
    slope = slope_ref[0]
    lane = lax.broadcasted_iota(jnp.int32, (1, L), 1)
    t_i = s0 + (lane & (Q_BLOCK - 1))

    sc = lax.dot_general(kc_ref[0], qs, _NT, preferred_element_type=_F32)
    n_io = lax.broadcasted_iota(jnp.int32, (nh, L), 0)
    valid = n_io <= ((t_i - (CMP_BLOCK - 1)) >> CMP_STRIDE_LOG2)
    s = jnp.where(valid, sc + cb_ref[0], NEG_INF)
    m = jnp.max(s, axis=0, keepdims=True)
    e = jnp.where(valid, jnp.exp(s - m), 0.0)
    l = jnp.sum(e, axis=0, keepdims=True)
    p = e * (1.0 / jnp.maximum(l, 1e-30))
    o_cmp = jnp.dot(vct_ref[0], p.astype(_BF16), preferred_element_type=_F32)

    ps = p[:, 0:Q_BLOCK]
    for r in range(1, R):
        ps = ps + p[:, r * Q_BLOCK:(r + 1) * Q_BLOCK]
    hi = ps.astype(_BF16)
    r1 = ps - hi.astype(_F32)
    mid = r1.astype(_BF16)
    lo = (r1 - mid.astype(_F32)).astype(_BF16)
    ovt = ovt_ref[...]
    imp = (jnp.dot(ovt, hi, preferred_element_type=_F32) + jnp.dot(ovt, mid, preferred_element_type=_F32)
           + jnp.dot(ovt, lo, preferred_element_type=_F32))

    j_io = lax.broadcasted_iota(jnp.int32, (n_slc, Q_BLOCK), 0)
    tq = s0 + lax.broadcasted_iota(jnp.int32, (1, Q_BLOCK), 1)
    cur = tq >> SLC_BLOCK_LOG2
    valid_s = (j_io * SLC_BLOCK) <= tq
    forced = (j_io == 0) | (j_io == cur) | (j_io == cur - 1)
    score = jnp.where(valid_s, imp + jnp.where(forced, FORCE_BONUS, 0.0), -1.0)

    def pick(_, carry):
        work, sel = carry
        mx = jnp.max(work, axis=0, keepdims=True)
        first = jnp.min(jnp.where(work == mx, j_io, n_slc), axis=0, keepdims=True)
        hit = j_io == first
        return jnp.where(hit, -jnp.inf, work), jnp.where(hit, 1.0, sel)

    _, sel = lax.fori_loop(0, n_sel, pick, (score, jnp.zeros((n_slc, Q_BLOCK), _F32)))
    sel_ref[...] = sel
    for k in range(n_slc // SLC_PER_TILE):
        flag_ref[k] = (jnp.max(sel[k * SLC_PER_TILE:(k + 1) * SLC_PER_TILE, :]) > 0.0).astype(jnp.int32)

    m_ref[...] = jnp.full((1, L), NEG_INF, _F32)
    l_ref[...] = jnp.zeros((1, L), _F32)
    acc_ref[...] = jnp.zeros((HEAD_DIM, L), _F32)
    key_io = lax.broadcasted_iota(jnp.int32, (SLC_TILE, L), 0)
    last = (s0 + Q_BLOCK - 1) // SLC_TILE

    def tile_update(kt, causal):
        selt = sel_ref[pl.ds(pl.multiple_of(kt * SLC_PER_TILE, SLC_PER_TILE), SLC_PER_TILE), :]
        sx = jnp.broadcast_to(selt[:, None, :], (SLC_PER_TILE, SLC_BLOCK, Q_BLOCK)).reshape(SLC_TILE, Q_BLOCK)
        ok = jnp.concatenate([sx] * R, axis=1) > 0.0
        if causal:
            ok = ok & (key_io <= t_i - kt * SLC_TILE)
        sr = lax.dot_general(ks_ref[0, kt], qs, _NT, preferred_element_type=_F32)
        st = jnp.where(ok, sr + rb_ref[0, 0:SLC_TILE, :], NEG_INF)
        ck = slope * (kt * SLC_TILE).astype(_F32)
        m_old = m_ref[...]
        m_new = jnp.maximum(m_old, jnp.max(st, axis=0, keepdims=True) + ck)
        alpha = jnp.exp(m_old - m_new)
        et = jnp.exp(st - (m_new - ck))
        l_ref[...] = alpha * l_ref[...] + jnp.sum(et, axis=0, keepdims=True)
        acc_ref[...] = alpha * acc_ref[...] + jnp.dot(vst_ref[0, kt], et.astype(_BF16),
                                                      preferred_element_type=_F32)
        m_ref[...] = m_new

    def slc_tile(kt, carry):
        @pl.when(flag_ref[kt] > 0)
        def _():
            tile_update(kt, False)
        return carry

    lax.fori_loop(0, last, slc_tile, 0)
    tile_update(last, True)
    o_slc = acc_ref[...] * (1.0 / l_ref[...])

    start = pl.multiple_of(jnp.maximum(s0 - WINDOW, 0), Q_BLOCK)
    sw = lax.dot_general(kw_ref[0, pl.ds(start, WIN_KEYS), :], qs, _NT, preferred_element_type=_F32)
    row_io = lax.broadcasted_iota(jnp.int32, (WIN_KEYS, L), 0)
    rel = t_i - start
    okw = (row_io <= rel) & (row_io > rel - WINDOW)
    s = jnp.where(okw, sw + rb_ref[0], NEG_INF)
    m = jnp.max(s, axis=0, keepdims=True)
    e = jnp.exp(s - m)
    pw = (e * (1.0 / jnp.sum(e, axis=0, keepdims=True))).astype(_BF16)
    b0 = start // Q_BLOCK
    o_win = jnp.dot(vwt_ref[0, b0], pw[0:Q_BLOCK], preferred_element_type=_F32)
    for i in range(1, WIN_BLOCKS):
        o_win = o_win + jnp.dot(vwt_ref[0, b0 + i], pw[i * Q_BLOCK:(i + 1) * Q_BLOCK],
                                preferred_element_type=_F32)

    sg = jax.nn.sigmoid(gl_ref[0, 0])
    o = sg[0:1] * o_cmp + sg[1:2] * o_slc + sg[2:3] * o_win
    for r in range(R):
        o_ref[:, r * HEAD_DIM:(r + 1) * HEAD_DIM] = o[:, r * Q_BLOCK:(r + 1) * Q_BLOCK].T


def nsa(proj, projb, kc, vc):
    S = proj.shape[0]
    G, R = N_KV_GROUPS, HEADS_PER_GROUP
    nh = S // CMP_STRIDE
    n_slc = S // SLC_BLOCK
    n_qb = S // Q_BLOCK
    n_kt = S // SLC_TILE
    n_sel = min(N_SELECT, n_slc)
    assert n_sel >= 3 and S >= WIN_KEYS
    L = R * Q_BLOCK

    def grp(off):
        return projb[:, off:off + KV_WIDTH].reshape(S, G, HEAD_DIM).transpose(1, 0, 2)

    kcb = kc.astype(_BF16)
    vct = vc.transpose(0, 2, 1).astype(_BF16)
    ks3 = grp(OFF_KS).reshape(G, n_kt, SLC_TILE, HEAD_DIM)
    vst3 = grp(OFF_VS).reshape(G, n_kt, SLC_TILE, HEAD_DIM).transpose(0, 1, 3, 2)
    kw = grp(OFF_KW)
    vwt3 = grp(OFF_VW).reshape(G, n_qb, Q_BLOCK, HEAD_DIM).transpose(0, 1, 3, 2)
    gl = proj[:, OFF_GATE:OFF_GATE + N_ATT_HEADS * N_BRANCH].reshape(n_qb, Q_BLOCK, G, R, N_BRANCH)
    gl = gl.transpose(2, 0, 4, 3, 1).reshape(G, n_qb, N_BRANCH, L)
    head = np.arange(N_ATT_HEADS, dtype=np.float64).reshape(G, R)
    slopes = np.repeat(2.0 ** (-8.0 * (head + 1) / N_ATT_HEADS), Q_BLOCK, axis=1).reshape(G, 1, L)
    slopes = jnp.asarray(slopes, _F32)
    cpos = jnp.arange(nh, dtype=_F32) * CMP_STRIDE + (CMP_BLOCK - 1) / 2.0
    cmp_bias = slopes * cpos[None, :, None]
    row_bias = slopes * jnp.arange(WIN_KEYS, dtype=_F32)[None, :, None]
    ci = np.arange(nh)[None, :] * CMP_STRIDE
    sj = np.arange(n_slc)[:, None] * SLC_BLOCK
    ovt = ((ci < sj + SLC_BLOCK) & (ci + CMP_BLOCK > sj) & (np.arange(nh)[None, :] < nh - 1))

    return pl.pallas_call(
        functools.partial(_nsa_kernel, n_sel=n_sel),
        grid=(G, n_qb),
        in_specs=[
            pl.BlockSpec((Q_BLOCK, L), lambda g, i: (i, g)),
            pl.BlockSpec((1, nh, HEAD_DIM), lambda g, i: (g, 0, 0)),
            pl.BlockSpec((1, HEAD_DIM, nh), lambda g, i: (g, 0, 0)),
            pl.BlockSpec((1, n_kt, SLC_TILE, HEAD_DIM), lambda g, i: (g, 0, 0, 0)),
            pl.BlockSpec((1, n_kt, HEAD_DIM, SLC_TILE), lambda g, i: (g, 0, 0, 0)),
            pl.BlockSpec((1, S, HEAD_DIM), lambda g, i: (g, 0, 0)),
            pl.BlockSpec((1, n_qb, HEAD_DIM, Q_BLOCK), lambda g, i: (g, 0, 0, 0)),
            pl.BlockSpec((1, 1, N_BRANCH, L), lambda g, i: (g, i, 0, 0)),
            pl.BlockSpec((1, 1, L), lambda g, i: (g, 0, 0)),
            pl.BlockSpec((1, nh, L), lambda g, i: (g, 0, 0)),
            pl.BlockSpec((1, WIN_KEYS, L), lambda g, i: (g, 0, 0)),
            pl.BlockSpec((n_slc, nh), lambda g, i: (0, 0)),
        ],
        out_specs=pl.BlockSpec((Q_BLOCK, L), lambda g, i: (i, g)),
        out_shape=jax.ShapeDtypeStruct((S, ATT_WIDTH), _F32),
        scratch_shapes=[pltpu.VMEM((n_slc, Q_BLOCK), _F32),
                        pltpu.VMEM((1, L), _F32),
                        pltpu.VMEM((1, L), _F32),
                        pltpu.VMEM((HEAD_DIM, L), _F32),
                        pltpu.SMEM((n_kt,), jnp.int32)],
        compiler_params=_cparams(("arbitrary", "arbitrary")),
        name="nsa",
    )(proj, kcb, vct, ks3, vst3, kw, vwt3, gl, slopes, cmp_bias, row_bias, jnp.asarray(ovt, _BF16))


def _layernorm_rows(r, g, b):
    mu = jnp.mean(r, axis=-1, keepdims=True)
    c = r - mu
    var = jnp.mean(c * c, axis=-1, keepdims=True)
    return c * lax.rsqrt(var + LN_EPS) * g + b


def _mix_out_kernel(x_ref, att_ref, u_ref, gb_ref, gc_ref, uh_ref, gch_ref, cw_ref, hg_ref, wo_ref,
                    g1_ref, b1_ref, h_ref):
    i = pl.program_id(0)
    tm = x_ref.shape[0]
    z = gc_ref[...] * u_ref[...]
    zh = jnp.where(i > 0, gch_ref[...] * uh_ref[...], 0.0)
    zz = jnp.concatenate([zh, z], axis=0)
    z1 = pltpu.roll(zz, 1, 0)[SUBLANES:]
    z2 = pltpu.roll(zz, 2, 0)[SUBLANES:]
    y = cw_ref[0:1, :] * z2 + cw_ref[1:2, :] * z1 + cw_ref[2:3, :] * z
    o_conv = gb_ref[...] * y
    mix = jnp.concatenate([att_ref[...], o_conv], axis=1)
    parts = []
    for k in range(N_MIX_GROUPS):
        blk = mix[:, k * HEAD_DIM:(k + 1) * HEAD_DIM]
        ms = jnp.mean(blk * blk, axis=-1, keepdims=True)
        parts.append(blk * lax.rsqrt(ms + RMS_EPS))
    mixn = (jnp.concatenate(parts, axis=1) * hg_ref[...]).astype(_BF16)
    r = ALPHA * x_ref[...] + jnp.dot(mixn, wo_ref[...], preferred_element_type=_F32)
    h_ref[...] = _layernorm_rows(r, g1_ref[...], b1_ref[...])


def mix_out(x2, o_att, proj, conv_w, head_norm_g, w_out, ln_g, ln_b):
    S = x2.shape[0]
    tm = min(256, S)
    hb = tm // SUBLANES
    cblk = lambda off: off // CONV_WIDTH
    halo = lambda off: pl.BlockSpec((SUBLANES, CONV_WIDTH), lambda i: (jnp.maximum(i * hb - 1, 0), cblk(off)))
    tile = lambda off: pl.BlockSpec((tm, CONV_WIDTH), lambda i: (i, cblk(off)))
    full = lambda shape: pl.BlockSpec(shape, lambda i: (0, 0))
    cw = jnp.zeros((SUBLANES, CONV_WIDTH), _F32).at[:CONV_K].set(conv_w)
    return pl.pallas_call(
        _mix_out_kernel,
        grid=(S // tm,),
        in_specs=[pl.BlockSpec((tm, D_MODEL), lambda i: (i, 0)),
                  pl.BlockSpec((tm, ATT_WIDTH), lambda i: (i, 0)),
                  tile(OFF_U), tile(OFF_GB), tile(OFF_GC), halo(OFF_U), halo(OFF_GC),
                  full((SUBLANES, CONV_WIDTH)), full((1, MIX_WIDTH)), full((MIX_WIDTH, D_MODEL)),
                  full((1, D_MODEL)), full((1, D_MODEL))],
        out_specs=pl.BlockSpec((tm, D_MODEL), lambda i: (i, 0)),
        out_shape=jax.ShapeDtypeStruct((S, D_MODEL), _F32),
        compiler_params=_cparams(("arbitrary",)),
        name="mix_out",
    )(x2, o_att, proj, proj, proj, proj, proj, cw, head_norm_g.reshape(1, MIX_WIDTH), w_out.astype(_BF16),
      ln_g.reshape(1, D_MODEL), ln_b.reshape(1, D_MODEL))


def _topk_rows(x, k):
    R, L = x.shape
    io = lax.broadcasted_iota(jnp.int32, (R, L), 0)
    ko = lax.broadcasted_iota(jnp.int32, (k, L), 0)
    vals = jnp.zeros((k, L), _F32)
    idxs = jnp.zeros((k, L), jnp.int32)
    for i in range(k):
        mx = jnp.max(x, axis=0, keepdims=True)
        first = jnp.min(jnp.where(x == mx, io, R), axis=0, keepdims=True)
        x = jnp.where(io == first, -jnp.inf, x)
        vals = jnp.where(ko == i, mx, vals)
        idxs = jnp.where(ko == i, first, idxs)
    return vals, idxs


def _take_rows(table, idx):
    out = jnp.zeros(idx.shape, table.dtype)
    for a in range(table.shape[0]):
        out = jnp.where(idx == a, table[a:a + 1, :], out)
    return out


def _peer_route_kernel(h_ref, wq_ref, sk_ref, eid_ref, g_ref):
    qp = jnp.dot(h_ref[...].astype(_BF16), wq_ref[...], preferred_element_type=_F32)
    for hd in range(PEER_HEADS):
        tops = []
        for c in range(2):
            j = hd * 2 + c
            qhc = qp[:, j * PEER_HALF:(j + 1) * PEER_HALF].astype(_BF16)
            st = lax.dot_general(sk_ref[j], qhc, _NT, preferred_element_type=_F32)
            tops.append(_topk_rows(st, PEER_TOPK))
        (v1, i1), (v2, i2) = tops
        nb = [PEER_TOPK // (a + 1) for a in range(PEER_TOPK)]
        tt = i1.shape[1]
        pad = -sum(nb) % SUBLANES
        cand = jnp.concatenate([v1[a:a + 1, :] + v2[0:nb[a], :] for a in range(PEER_TOPK)]
                               + [jnp.full((pad, tt), -jnp.inf, _F32)], axis=0)
        bs, bi = _topk_rows(cand, PEER_TOPK)
        zpad = [jnp.zeros((pad, tt), jnp.int32)]
        t1 = jnp.concatenate([jnp.broadcast_to(i1[a:a + 1, :], (nb[a], tt)) for a in range(PEER_TOPK)] + zpad,
                             axis=0)
        t2 = jnp.concatenate([i2[0:nb[a], :] for a in range(PEER_TOPK)] + zpad, axis=0)
        e1 = _take_rows(t1, bi)
        e2 = _take_rows(t2, bi)
        ex = jnp.exp(bs - jnp.max(bs, axis=0, keepdims=True))
        eid_ref[hd * PEER_TOPK:(hd + 1) * PEER_TOPK, :] = e1 * N_KEYS + e2
        g_ref[hd * PEER_TOPK:(hd + 1) * PEER_TOPK, :] = ex * (1.0 / jnp.sum(ex, axis=0, keepdims=True))


def peer_route(h, w_query, sub_keys):
    S = h.shape[0]
    tt = LANES
    skb = sub_keys.reshape(PEER_HEADS * 2, N_KEYS, PEER_HALF).astype(_BF16)
    return pl.pallas_call(
        _peer_route_kernel,
        grid=(S // tt,),
        in_specs=[pl.BlockSpec((tt, D_MODEL), lambda i: (i, 0)),
                  pl.BlockSpec((D_MODEL, PEER_HEADS * 2 * PEER_HALF), lambda i: (0, 0)),
                  pl.BlockSpec((PEER_HEADS * 2, N_KEYS, PEER_HALF), lambda i: (0, 0, 0))],
        out_specs=[pl.BlockSpec((PEER_K, tt), lambda i: (0, i)),
                   pl.BlockSpec((PEER_K, tt), lambda i: (0, i))],
        out_shape=[jax.ShapeDtypeStruct((PEER_K, S), jnp.int32),
                   jax.ShapeDtypeStruct((PEER_K, S), _F32)],
        compiler_params=_cparams(("arbitrary",)),
        name="peer_route",
    )(h, w_query.astype(_BF16), skb)


def _row_copy(uv_hbm, buf, sem, e, r):
    return pltpu.make_async_copy(uv_hbm.at[e], buf.at[r], sem)


def _wait_all_rows(buf, sem):
    pltpu.make_async_copy(buf, buf, sem).wait()


def _sum_lane_groups(x):
    n = x.shape[1]
    lane = lax.broadcasted_iota(jnp.int32, x.shape, 1)
    sh = 1
    while sh < D_CHUNKS:
        lower = pltpu.roll(x, sh, 1)
        upper = pltpu.roll(x, n - sh, 1)
        x = x + jnp.where((lane & sh) == 0, upper, lower)
        sh *= 2
    return x


def _expand_gates(g, ex):
    hi = g.astype(ex.dtype)
    r1 = g - hi.astype(_F32)
    mid = r1.astype(ex.dtype)
    lo = (r1 - mid.astype(_F32)).astype(ex.dtype)
    return (jnp.dot(hi, ex, preferred_element_type=_F32) + jnp.dot(mid, ex, preferred_element_type=_F32)
            + jnp.dot(lo, ex, preferred_element_type=_F32))


def _peer_batch(src, h_ref, o_ref, t0, gx, g2, b2):
    wide = PEER_K * D_CHUNKS
    diag = (lax.broadcasted_iota(jnp.int32, (D_CHUNKS, wide), 1) & (D_CHUNKS - 1)) == \
        lax.broadcasted_iota(jnp.int32, (D_CHUNKS, wide), 0)
    tiles = []
    for t in range(PEER_TB):
        tiles.append(jnp.concatenate([h_ref[t0 + t:t0 + t + 1, s * LANES:(s + 1) * LANES]
                                      for s in range(D_CHUNKS)], axis=0))
    parts = []
    for t in range(PEER_TB):
        ut = src[t * PEER_K:(t + 1) * PEER_K, 0:D_CHUNKS, :].reshape(wide, LANES)
        y = lax.dot_general(tiles[t].astype(_BF16), ut, _NT, preferred_element_type=_F32)
        parts.append(jnp.sum(jnp.where(diag, y, 0.0), axis=0, keepdims=True))
    a = _sum_lane_groups(jnp.concatenate(parts, axis=0))
    c = jax.nn.gelu(a) * gx
    for t in range(PEER_TB):
        lt = jnp.where(diag, c[t:t + 1, :], 0.0).astype(_BF16)
        vt = src[t * PEER_K:(t + 1) * PEER_K, D_CHUNKS:2 * D_CHUNKS, :].reshape(wide, LANES)
        r = ALPHA * tiles[t] + jnp.dot(lt, vt, preferred_element_type=_F32)
        mu = jnp.mean(r, axis=(0, 1), keepdims=True)
        d = r - mu
        var = jnp.mean(d * d, axis=(0, 1), keepdims=True)
        y = d * lax.rsqrt(var + LN_EPS) * g2 + b2
        for s in range(D_CHUNKS):
            o_ref[t0 + t:t0 + t + 1, s * LANES:(s + 1) * LANES] = y[s:s + 1, :]


def _peer_apply_kernel(eid_cur_ref, eid_nxt_ref, h_ref, g_ref, ex_ref, g2_ref, b2_ref, uv_hbm, o_ref,
                       buf_a, buf_b, sem):
    i = pl.program_id(0)
    n = pl.num_programs(0)
    gx = _expand_gates(g_ref[...], ex_ref[...])

    @pl.when(i == 0)
    def _():
        def body(c, carry):
            for j in range(ISSUE_UNROLL):
                r = c * ISSUE_UNROLL + j
                _row_copy(uv_hbm, buf_a, sem.at[0], eid_cur_ref[0, 0, r], r).start()
            return carry
        lax.fori_loop(0, PEER_ROWS // ISSUE_UNROLL, body, 0)

    _wait_all_rows(buf_a, sem.at[0])
    for r in range(PEER_ROWS):
        _row_copy(uv_hbm, buf_b, sem.at[1], eid_cur_ref[0, 0, PEER_ROWS + r], r).start(priority=r % 2)
    _peer_batch(buf_a, h_ref, o_ref, 0, gx[0:PEER_TB], g2_ref[...], b2_ref[...])

    _wait_all_rows(buf_b, sem.at[1])
    for r in range(PEER_ROWS):
        _row_copy(uv_hbm, buf_a, sem.at[0], eid_nxt_ref[0, 0, r], r).start(priority=r % 2)
    _peer_batch(buf_b, h_ref, o_ref, PEER_TB, gx[PEER_TB:2 * PEER_TB], g2_ref[...], b2_ref[...])

    @pl.when(i == n - 1)
    def _():
        _wait_all_rows(buf_a, sem.at[0])


def pack_expert_rows(expert_u, expert_v):
    E = expert_u.shape[0]
    tile = lambda w: w.astype(_BF16).reshape(E, D_CHUNKS, LANES)
    return jnp.concatenate([tile(expert_u), tile(expert_v)], axis=1)


def peer_apply(h, eid, gates, uvp, ln_g, ln_b):
    S = h.shape[0]
    npair = S // (2 * PEER_TB)
    eid3 = eid.reshape(npair, 1, 2 * PEER_ROWS)
    wide = PEER_K * D_CHUNKS
    expand = (np.arange(wide)[None, :] // D_CHUNKS) == np.arange(PEER_K)[:, None]
    gbuf = pltpu.VMEM((PEER_ROWS, 2 * D_CHUNKS, LANES), uvp.dtype)
    return pl.pallas_call(
        _peer_apply_kernel,
        grid=(npair,),
        in_specs=[
            pl.BlockSpec((1, 1, 2 * PEER_ROWS), lambda i: (i, 0, 0), memory_space=pltpu.SMEM),
            pl.BlockSpec((1, 1, 2 * PEER_ROWS), lambda i: (jnp.minimum(i + 1, npair - 1), 0, 0),
                         memory_space=pltpu.SMEM),
            pl.BlockSpec((2 * PEER_TB, D_MODEL), lambda i: (i, 0)),
            pl.BlockSpec((2 * PEER_TB, PEER_K), lambda i: (i, 0)),
            pl.BlockSpec((PEER_K, wide), lambda i: (0, 0)),
            pl.BlockSpec((D_CHUNKS, LANES), lambda i: (0, 0)),
            pl.BlockSpec((D_CHUNKS, LANES), lambda i: (0, 0)),
            pl.BlockSpec(memory_space=pl.ANY),
        ],
        out_specs=pl.BlockSpec((2 * PEER_TB, D_MODEL), lambda i: (i, 0)),
        out_shape=jax.ShapeDtypeStruct((S, D_MODEL), _F32),
        scratch_shapes=[gbuf, gbuf, pltpu.SemaphoreType.DMA((2,))],
        compiler_params=_cparams(("arbitrary",)),
        name="peer_apply",
    )(eid3, eid3, h, gates, jnp.asarray(expand, _BF16), ln_g.reshape(D_CHUNKS, LANES),
      ln_b.reshape(D_CHUNKS, LANES), uvp)


def _pad_w_in(w):
    offs = np.cumsum([0, ATT_WIDTH] + [KV_WIDTH] * 6 + [N_ATT_HEADS * N_BRANCH] + [CONV_WIDTH] * 3)
    seg = lambda k: w[:, offs[k]:offs[k + 1]]
    q, kc, vc, ks, vs, kw, vw, gate, u, gb, gc = [seg(k) for k in range(11)]
    gate = jnp.pad(gate, ((0, 0), (0, GATE_PAD - gate.shape[1])))
    return jnp.concatenate([q, u, gb, gc, kc, vc, ks, vs, kw, vw, gate], axis=1)


def _layer(x2, w_in, cmp_k, cmp_v, conv_w, head_norm_g, w_out, ln1, w_query, sub_keys, expert_u, expert_v, ln2):
    S = x2.shape[0]
    G = N_KV_GROUPS
    proj, projb = in_proj(x2.astype(_BF16), _pad_w_in(w_in).astype(_BF16))

    def half_blocks(off):
        t = proj[:, off:off + KV_WIDTH].reshape(S, G, HEAD_DIM).transpose(1, 0, 2)
        return t.reshape(G, S // CMP_STRIDE, CMP_STRIDE * HEAD_DIM)

    kc = compress(half_blocks(OFF_KC), *cmp_k)
    vc = compress(half_blocks(OFF_VC), *cmp_v)
    o_att = nsa(proj, projb, kc, vc)
    h = mix_out(x2, o_att, proj, conv_w, head_norm_g, w_out, *ln1)
    eid_t, g_t = peer_route(h, w_query, sub_keys)
    return peer_apply(h, eid_t.T, g_t.T, pack_expert_rows(expert_u, expert_v), *ln2)


def kernel(x, w_in, cmp_k_pos, cmp_k_w1, cmp_k_b1, cmp_k_w2, cmp_v_pos, cmp_v_w1, cmp_v_b1, cmp_v_w2, conv_w, head_norm_g, w_out, ln1_g, ln1_b, w_query, sub_keys, expert_u, expert_v, ln2_g, ln2_b):
    B, S, D = x.shape
    outs = []
    for b in range(B):
        xb = x[b]
        for l in range(w_in.shape[0]):
            xb = _layer(xb, w_in[l],
                        (cmp_k_pos[l], cmp_k_w1[l], cmp_k_b1[l], cmp_k_w2[l]),
                        (cmp_v_pos[l], cmp_v_w1[l], cmp_v_b1[l], cmp_v_w2[l]),
                        conv_w[l], head_norm_g[l], w_out[l], (ln1_g[l], ln1_b[l]),
                        w_query[l], sub_keys[l], expert_u[l], expert_v[l], (ln2_g[l], ln2_b[l]))
        outs.append(xb)
    return jnp.stack(outs, axis=0)
```

```python
import functools

import numpy as np
import jax
import jax.numpy as jnp
from jax import lax
from jax.experimental import pallas as pl
from jax.experimental.pallas import tpu as pltpu

D_MODEL = 2048
HEAD_DIM = 128
N_ATT_HEADS = 8
N_KV_GROUPS = 2
HEADS_PER_GROUP = N_ATT_HEADS // N_KV_GROUPS
ATT_WIDTH = N_ATT_HEADS * HEAD_DIM
KV_WIDTH = N_KV_GROUPS * HEAD_DIM
CONV_WIDTH = 1024
MIX_WIDTH = ATT_WIDTH + CONV_WIDTH
N_MIX_GROUPS = MIX_WIDTH // HEAD_DIM
N_BRANCH = 3
CMP_BLOCK = 32
CMP_STRIDE = 16
CMP_HIDDEN = 512
SLC_BLOCK = 64
N_SELECT = 16
WINDOW = 512
Q_BLOCK = 128
CONV_K = 3
PEER_HEADS = 8
N_KEYS = 128
PEER_HALF = 128
PEER_TOPK = 16
PEER_K = PEER_HEADS * PEER_TOPK
DEPTH = 1
ALPHA = (2.0 * DEPTH) ** 0.25
LN_EPS = 1e-5
RMS_EPS = 1e-6
NEG_INF = -1e30
FORCE_BONUS = 1e4

LANES = 128
SUBLANES = 8
VMEM_LIMIT_BYTES = 56 * 1024 * 1024

GATE_PAD = LANES
OFF_Q = 0
OFF_U = OFF_Q + ATT_WIDTH
OFF_GB = OFF_U + CONV_WIDTH
OFF_GC = OFF_GB + CONV_WIDTH
OFF_KC = OFF_GC + CONV_WIDTH
OFF_VC = OFF_KC + KV_WIDTH
OFF_KS = OFF_VC + KV_WIDTH
OFF_VS = OFF_KS + KV_WIDTH
OFF_KW = OFF_VS + KV_WIDTH
OFF_VW = OFF_KW + KV_WIDTH
OFF_GATE = OFF_VW + KV_WIDTH
PROJ_W = OFF_GATE + GATE_PAD

SLC_TILE = 512
SLC_PER_TILE = SLC_TILE // SLC_BLOCK
SLC_BLOCK_LOG2 = SLC_BLOCK.bit_length() - 1
CMP_STRIDE_LOG2 = CMP_STRIDE.bit_length() - 1
assert 1 << SLC_BLOCK_LOG2 == SLC_BLOCK and 1 << CMP_STRIDE_LOG2 == CMP_STRIDE
WIN_KEYS = WINDOW + Q_BLOCK
WIN_BLOCKS = WIN_KEYS // Q_BLOCK

PEER_TB = 8
PEER_ROWS = PEER_TB * PEER_K
D_CHUNKS = D_MODEL // LANES
PEER_TPB = 2
assert D_CHUNKS & (D_CHUNKS - 1) == 0 and PEER_TB % PEER_TPB == 0

_BF16 = jnp.bfloat16
_F32 = jnp.float32
_NT = (((1,), (1,)), ((), ()))


def _cparams(sem):
    return pltpu.CompilerParams(dimension_semantics=sem, vmem_limit_bytes=VMEM_LIMIT_BYTES)


def _matmul_kernel(x_ref, w_ref, o_ref, ob_ref):
    acc = jnp.dot(x_ref[...], w_ref[...], preferred_element_type=_F32)
    o_ref[...] = acc
    ob_ref[...] = acc.astype(ob_ref.dtype)


def in_proj(xb, wb):
    S, K = xb.shape
    N = wb.shape[1]
    tm = min(512, S)
    tn = N // 5
    return pl.pallas_call(
        _matmul_kernel,
        grid=(N // tn, S // tm),
        in_specs=[pl.BlockSpec((tm, K), lambda j, i: (i, 0)),
                  pl.BlockSpec((K, tn), lambda j, i: (0, j))],
        out_specs=[pl.BlockSpec((tm, tn), lambda j, i: (i, j)),
                   pl.BlockSpec((tm, tn), lambda j, i: (i, j))],
        out_shape=[jax.ShapeDtypeStruct((S, N), _F32), jax.ShapeDtypeStruct((S, N), wb.dtype)],
        compiler_params=_cparams(("arbitrary", "arbitrary")),
        name="in_proj",
    )(xb, wb)


def _compress_kernel(hb_ref, pos_ref, w1_ref, b1_ref, w2_ref, o_ref):
    nh = hb_ref.shape[1]
    half = hb_ref.shape[2]
    hb = hb_ref[0]
    top = (hb + pos_ref[:, :half]).astype(_BF16)
    bot = (hb + pos_ref[:, half:]).astype(_BF16)
    a = jnp.dot(top, w1_ref[:half, :], preferred_element_type=_F32)
    b = jnp.dot(bot, w1_ref[half:, :], preferred_element_type=_F32)
    hidden = a + pltpu.roll(b, nh - 1, 0) + b1_ref[...]
    act = jax.nn.gelu(hidden).astype(_BF16)
    o_ref[0] = jnp.dot(act, w2_ref[...], preferred_element_type=_F32)


def compress(hb, pos, w1, b1, w2):
    G, nh, half = hb.shape
    posflat = pos.reshape(1, CMP_BLOCK * HEAD_DIM)
    return pl.pallas_call(
        _compress_kernel,
        grid=(G,),
        in_specs=[pl.BlockSpec((1, nh, half), lambda g: (g, 0, 0)),
                  pl.BlockSpec((1, 2 * half), lambda g: (0, 0)),
                  pl.BlockSpec((2 * half, CMP_HIDDEN), lambda g: (0, 0)),
                  pl.BlockSpec((1, CMP_HIDDEN), lambda g: (0, 0)),
                  pl.BlockSpec((CMP_HIDDEN, HEAD_DIM), lambda g: (0, 0))],
        out_specs=pl.BlockSpec((1, nh, HEAD_DIM), lambda g: (g, 0, 0)),
        out_shape=jax.ShapeDtypeStruct((G, nh, HEAD_DIM), _F32),
        compiler_params=_cparams(("arbitrary",)),
        name="compress",
    )(hb, posflat, w1.astype(_BF16), b1.reshape(1, CMP_HIDDEN), w2.astype(_BF16))


def _nsa_kernel(q_ref, kc_ref, vct_ref, ks_ref, vst_ref, kw_ref, vwt_ref, gl_ref, slope_ref, cb_ref, rb_ref,
                ovt_ref, o_ref, sel_ref, m_ref, l_ref, acc_ref, flag_ref, *, n_sel):
    nb = pl.program_id(1)
    s0 = nb * Q_BLOCK
    R = HEADS_PER_GROUP
    L = R * Q_BLOCK
    nh = kc_ref.shape[1]
    n_slc = ovt_ref.shape[0]
    scale = HEAD_DIM ** -0.5

    qs = jnp.concatenate([q_ref[:, r * HEAD_DIM:(r + 1) * HEAD_DIM] for r in range(R)], axis=0)
    qs = (qs * scale).astype(_BF16)
    slope = slope_ref[0]
    lane = lax.broadcasted_iota(jnp.int32, (1, L), 1)
    t_i = s0 + (lane & (Q_BLOCK - 1))

    sc = lax.dot_general(kc_ref[0], qs, _NT, preferred_element_type=_F32)
    n_io = lax.broadcasted_iota(jnp.int32, (nh, L), 0)
    valid = n_io <= ((t_i - (CMP_BLOCK - 1)) >> CMP_STRIDE_LOG2)
    s = jnp.where(valid, sc + cb_ref[0], NEG_INF)
    m = jnp.max(s, axis=0, keepdims=True)
    e = jnp.where(valid, jnp.exp(s - m), 0.0)
    l = jnp.sum(e, axis=0, keepdims=True)
    p = e * (1.0 / jnp.maximum(l, 1e-30))
    o_cmp = jnp.dot(vct_ref[0], p.astype(_BF16), preferred_element_type=_F32)

    ps = p[:, 0:Q_BLOCK]
    for r in range(1, R):
        ps = ps + p[:, r * Q_BLOCK:(r + 1) * Q_BLOCK]
    hi = ps.astype(_BF16)
    r1 = ps - hi.astype(_F32)
    mid = r1.astype(_BF16)
    lo = (r1 - mid.astype(_F32)).astype(_BF16)
    ovt = ovt_ref[...]
    imp = (jnp.dot(ovt, hi, preferred_element_type=_F32) + jnp.dot(ovt, mid, preferred_element_type=_F32)
           + jnp.dot(ovt, lo, preferred_element_type=_F32))

    j_io = lax.broadcasted_iota(jnp.int32, (n_slc, Q_BLOCK), 0)
    tq = s0 + lax.broadcasted_iota(jnp.int32, (1, Q_BLOCK), 1)
    cur = tq >> SLC_BLOCK_LOG2
    valid_s = (j_io * SLC_BLOCK) <= tq
    forced = (j_io == 0) | (j_io == cur) | (j_io == cur - 1)
    score = jnp.where(valid_s, imp + jnp.where(forced, FORCE_BONUS, 0.0), -1.0)

    def pick(_, carry):
        work, sel = carry
        mx = jnp.max(work, axis=0, keepdims=True)
        first = jnp.min(jnp.where(work == mx, j_io, n_slc), axis=0, keepdims=True)
        hit = j_io == first
        return jnp.where(hit, -jnp.inf, work), jnp.where(hit, 1.0, sel)

    _, sel = lax.fori_loop(0, n_sel, pick, (score, jnp.zeros((n_slc, Q_BLOCK), _F32)))
    sel_ref[...] = sel
    for k in range(n_slc // SLC_PER_TILE):
        flag_ref[k] = (jnp.max(sel[k * SLC_PER_TILE:(k + 1) * SLC_PER_TILE, :]) > 0.0).astype(jnp.int32)

    m_ref[...] = jnp.full((1, L), NEG_INF, _F32)
    l_ref[...] = jnp.zeros((1, L), _F32)
    acc_ref[...] = jnp.zeros((HEAD_DIM, L), _F32)
    key_io = lax.broadcasted_iota(jnp.int32, (SLC_TILE, L), 0)
    last = (s0 + Q_BLOCK - 1) // SLC_TILE

    def tile_update(kt, causal):
        selt = sel_ref[pl.ds(pl.multiple_of(kt * SLC_PER_TILE, SLC_PER_TILE), SLC_PER_TILE), :]
        sx = jnp.broadcast_to(selt[:, None, :], (SLC_PER_TILE, SLC_BLOCK, Q_BLOCK)).reshape(SLC_TILE, Q_BLOCK)
        ok = jnp.concatenate([sx] * R, axis=1) > 0.0
        if causal:
            ok = ok & (key_io <= t_i - kt * SLC_TILE)
        sr = lax.dot_general(ks_ref[0, kt], qs, _NT, preferred_element_type=_F32)
        st = jnp.where(ok, sr + rb_ref[0, 0:SLC_TILE, :], NEG_INF)
        ck = slope * (kt * SLC_TILE).astype(_F32)
        m_old = m_ref[...]
        m_new = jnp.maximum(m_old, jnp.max(st, axis=0, keepdims=True) + ck)
        alpha = jnp.exp(m_old - m_new)
        et = jnp.exp(st - (m_new - ck))
        l_ref[...] = alpha * l_ref[...] + jnp.sum(et, axis=0, keepdims=True)
        acc_ref[...] = alpha * acc_ref[...] + jnp.dot(vst_ref[0, kt], et.astype(_BF16),
                                                      preferred_element_type=_F32)
        m_ref[...] = m_new

    def slc_tile(kt, carry):
        @pl.when(flag_ref[kt] > 0)
        def _():
            tile_update(kt, False)
        return carry

    lax.fori_loop(0, last, slc_tile, 0)
    tile_update(last, True)
    o_slc = acc_ref[...] * (1.0 / l_ref[...])

    start = pl.multiple_of(jnp.maximum(s0 - WINDOW, 0), Q_BLOCK)
    sw = lax.dot_general(kw_ref[0, pl.ds(start, WIN_KEYS), :], qs, _NT, preferred_element_type=_F32)
    row_io = lax.broadcasted_iota(jnp.int32, (WIN_KEYS, L), 0)
    rel = t_i - start
    okw = (row_io <= rel) & (row_io > rel - WINDOW)
    s = jnp.where(okw, sw + rb_ref[0], NEG_INF)
    m = jnp.max(s, axis=0, keepdims=True)
    e = jnp.exp(s - m)
    pw = (e * (1.0 / jnp.sum(e, axis=0, keepdims=True))).astype(_BF16)
    b0 = start // Q_BLOCK
    o_win = jnp.dot(vwt_ref[0, b0], pw[0:Q_BLOCK], preferred_element_type=_F32)
    for i in range(1, WIN_BLOCKS):
        o_win = o_win + jnp.dot(vwt_ref[0, b0 + i], pw[i * Q_BLOCK:(i + 1) * Q_BLOCK],
                                preferred_element_type=_F32)

    sg = jax.nn.sigmoid(gl_ref[0, 0])
    o = sg[0:1] * o_cmp + sg[1:2] * o_slc + sg[2:3] * o_win
    for r in range(R):
        o_ref[:, r * HEAD_DIM:(r + 1) * HEAD_DIM] = o[:, r * Q_BLOCK:(r + 1) * Q_BLOCK].T


def nsa(proj, projb, kc, vc):
    S = proj.shape[0]
    G, R = N_KV_GROUPS, HEADS_PER_GROUP
    nh = S // CMP_STRIDE
    n_slc = S // SLC_BLOCK
    n_qb = S // Q_BLOCK
    n_kt = S // SLC_TILE
    n_sel = min(N_SELECT, n_slc)
    assert n_sel >= 3 and S >= WIN_KEYS
    L = R * Q_BLOCK

    def grp(off):
        return projb[:, off:off + KV_WIDTH].reshape(S, G, HEAD_DIM).transpose(1, 0, 2)

    kcb = kc.astype(_BF16)
    vct = vc.transpose(0, 2, 1).astype(_BF16)
    ks3 = grp(OFF_KS).reshape(G, n_kt, SLC_TILE, HEAD_DIM)
    vst3 = grp(OFF_VS).reshape(G, n_kt, SLC_TILE, HEAD_DIM).transpose(0, 1, 3, 2)
    kw = grp(OFF_KW)
    vwt3 = grp(OFF_VW).reshape(G, n_qb, Q_BLOCK, HEAD_DIM).transpose(0, 1, 3, 2)
    gl = proj[:, OFF_GATE:OFF_GATE + N_ATT_HEADS * N_BRANCH].reshape(n_qb, Q_BLOCK, G, R, N_BRANCH)
    gl = gl.transpose(2, 0, 4, 3, 1).reshape(G, n_qb, N_BRANCH, L)
    head = np.arange(N_ATT_HEADS, dtype=np.float64).reshape(G, R)
    slopes = np.repeat(2.0 ** (-8.0 * (head + 1) / N_ATT_HEADS), Q_BLOCK, axis=1).reshape(G, 1, L)
    slopes = jnp.asarray(slopes, _F32)
    cpos = jnp.arange(nh, dtype=_F32) * CMP_STRIDE + (CMP_BLOCK - 1) / 2.0
    cmp_bias = slopes * cpos[None, :, None]
    row_bias = slopes * jnp.arange(WIN_KEYS, dtype=_F32)[None, :, None]
    ci = np.arange(nh)[None, :] * CMP_STRIDE
    sj = np.arange(n_slc)[:, None] * SLC_BLOCK
    ovt = ((ci < sj + SLC_BLOCK) & (ci + CMP_BLOCK > sj) & (np.arange(nh)[None, :] < nh - 1))

    return pl.pallas_call(
        functools.partial(_nsa_kernel, n_sel=n_sel),
        grid=(G, n_qb),
        in_specs=[
            pl.BlockSpec((Q_BLOCK, L), lambda g, i: (i, g)),
            pl.BlockSpec((1, nh, HEAD_DIM), lambda g, i: (g, 0, 0)),
            pl.BlockSpec((1, HEAD_DIM, nh), lambda g, i: (g, 0, 0)),
            pl.BlockSpec((1, n_kt, SLC_TILE, HEAD_DIM), lambda g, i: (g, 0, 0, 0)),
            pl.BlockSpec((1, n_kt, HEAD_DIM, SLC_TILE), lambda g, i: (g, 0, 0, 0)),
            pl.BlockSpec((1, S, HEAD_DIM), lambda g, i: (g, 0, 0)),
            pl.BlockSpec((1, n_qb, HEAD_DIM, Q_BLOCK), lambda g, i: (g, 0, 0, 0)),
            pl.BlockSpec((1, 1, N_BRANCH, L), lambda g, i: (g, i, 0, 0)),
            pl.BlockSpec((1, 1, L), lambda g, i: (g, 0, 0)),
            pl.BlockSpec((1, nh, L), lambda g, i: (g, 0, 0)),
            pl.BlockSpec((1, WIN_KEYS, L), lambda g, i: (g, 0, 0)),
            pl.BlockSpec((n_slc, nh), lambda g, i: (0, 0)),
        ],
        out_specs=pl.BlockSpec((Q_BLOCK, L), lambda g, i: (i, g)),
        out_shape=jax.ShapeDtypeStruct((S, ATT_WIDTH), _F32),
        scratch_shapes=[pltpu.VMEM((n_slc, Q_BLOCK), _F32),
                        pltpu.VMEM((1, L), _F32),
                        pltpu.VMEM((1, L), _F32),
                        pltpu.VMEM((HEAD_DIM, L), _F32),
                        pltpu.SMEM((n_kt,), jnp.int32)],
        compiler_params=_cparams(("arbitrary", "arbitrary")),
        name="nsa",
    )(proj, kcb, vct, ks3, vst3, kw, vwt3, gl, slopes, cmp_bias, row_bias, jnp.asarray(ovt, _BF16))


def _layernorm_rows(r, g, b):
    mu = jnp.mean(r, axis=-1, keepdims=True)
    c = r - mu
    var = jnp.mean(c * c, axis=-1, keepdims=True)
    return c * lax.rsqrt(var + LN_EPS) * g + b


def _mix_out_kernel(x_ref, att_ref, u_ref, gb_ref, gc_ref, uh_ref, gch_ref, cw_ref, hg_ref, wo_ref,
                    g1_ref, b1_ref, h_ref):
    i = pl.program_id(0)
    tm = x_ref.shape[0]
    z = gc_ref[...] * u_ref[...]
    zh = jnp.where(i > 0, gch_ref[...] * uh_ref[...], 0.0)
    zz = jnp.concatenate([zh, z], axis=0)
    z1 = pltpu.roll(zz, 1, 0)[SUBLANES:]
    z2 = pltpu.roll(zz, 2, 0)[SUBLANES:]
    y = cw_ref[0:1, :] * z2 + cw_ref[1:2, :] * z1 + cw_ref[2:3, :] * z
    o_conv = gb_ref[...] * y
    mix = jnp.concatenate([att_ref[...], o_conv], axis=1)
    parts = []
    for k in range(N_MIX_GROUPS):
        blk = mix[:, k * HEAD_DIM:(k + 1) * HEAD_DIM]
        ms = jnp.mean(blk * blk, axis=-1, keepdims=True)
        parts.append(blk * lax.rsqrt(ms + RMS_EPS))
    mixn = (jnp.concatenate(parts, axis=1) * hg_ref[...]).astype(_BF16)
    r = ALPHA * x_ref[...] + jnp.dot(mixn, wo_ref[...], preferred_element_type=_F32)
    h_ref[...] = _layernorm_rows(r, g1_ref[...], b1_ref[...])


def mix_out(x2, o_att, proj, conv_w, head_norm_g, w_out, ln_g, ln_b):
    S = x2.shape[0]
    tm = min(256, S)
    hb = tm // SUBLANES
    cblk = lambda off: off // CONV_WIDTH
    halo = lambda off: pl.BlockSpec((SUBLANES, CONV_WIDTH), lambda i: (jnp.maximum(i * hb - 1, 0), cblk(off)))
    tile = lambda off: pl.BlockSpec((tm, CONV_WIDTH), lambda i: (i, cblk(off)))
    full = lambda shape: pl.BlockSpec(shape, lambda i: (0, 0))
    cw = jnp.zeros((SUBLANES, CONV_WIDTH), _F32).at[:CONV_K].set(conv_w)
    return pl.pallas_call(
        _mix_out_kernel,
        grid=(S // tm,),
        in_specs=[pl.BlockSpec((tm, D_MODEL), lambda i: (i, 0)),
                  pl.BlockSpec((tm, ATT_WIDTH), lambda i: (i, 0)),
                  tile(OFF_U), tile(OFF_GB), tile(OFF_GC), halo(OFF_U), halo(OFF_GC),
                  full((SUBLANES, CONV_WIDTH)), full((1, MIX_WIDTH)), full((MIX_WIDTH, D_MODEL)),
                  full((1, D_MODEL)), full((1, D_MODEL))],
        out_specs=pl.BlockSpec((tm, D_MODEL), lambda i: (i, 0)),
        out_shape=jax.ShapeDtypeStruct((S, D_MODEL), _F32),
        compiler_params=_cparams(("arbitrary",)),
        name="mix_out",
    )(x2, o_att, proj, proj, proj, proj, proj, cw, head_norm_g.reshape(1, MIX_WIDTH), w_out.astype(_BF16),
      ln_g.reshape(1, D_MODEL), ln_b.reshape(1, D_MODEL))


def _topk_rows(x, k):
    R, L = x.shape
    io = lax.broadcasted_iota(jnp.int32, (R, L), 0)
    ko = lax.broadcasted_iota(jnp.int32, (k, L), 0)
    vals = jnp.zeros((k, L), _F32)
    idxs = jnp.zeros((k, L), jnp.int32)
    for i in range(k):
        mx = jnp.max(x, axis=0, keepdims=True)
        first = jnp.min(jnp.where(x == mx, io, R), axis=0, keepdims=True)
        x = jnp.where(io == first, -jnp.inf, x)
        vals = jnp.where(ko == i, mx, vals)
        idxs = jnp.where(ko == i, first, idxs)
    return vals, idxs


def _take_rows(table, idx):
    out = jnp.zeros(idx.shape, table.dtype)
    for a in range(table.shape[0]):
        out = jnp.where(idx == a, table[a:a + 1, :], out)
    return out


def _peer_route_kernel(h_ref, wq_ref, sk_ref, eid_ref, g_ref):
    qp = jnp.dot(h_ref[...].astype(_BF16), wq_ref[...], preferred_element_type=_F32)
    for hd in range(PEER_HEADS):
        tops = []
        for c in range(2):
            j = hd * 2 + c
            qhc = qp[:, j * PEER_HALF:(j + 1) * PEER_HALF].astype(_BF16)
            st = lax.dot_general(sk_ref[j], qhc, _NT, preferred_element_type=_F32)
            tops.append(_topk_rows(st, PEER_TOPK))
        (v1, i1), (v2, i2) = tops
        nb = [PEER_TOPK // (a + 1) for a in range(PEER_TOPK)]
        tt = i1.shape[1]
        pad = -sum(nb) % SUBLANES
        cand = jnp.concatenate([v1[a:a + 1, :] + v2[0:nb[a], :] for a in range(PEER_TOPK)]
                               + [jnp.full((pad, tt), -jnp.inf, _F32)], axis=0)
        bs, bi = _topk_rows(cand, PEER_TOPK)
        zpad = [jnp.zeros((pad, tt), jnp.int32)]
        t1 = jnp.concatenate([jnp.broadcast_to(i1[a:a + 1, :], (nb[a], tt)) for a in range(PEER_TOPK)] + zpad,
                             axis=0)
        t2 = jnp.concatenate([i2[0:nb[a], :] for a in range(PEER_TOPK)] + zpad, axis=0)
        e1 = _take_rows(t1, bi)
        e2 = _take_rows(t2, bi)
        ex = jnp.exp(bs - jnp.max(bs, axis=0, keepdims=True))
        eid_ref[hd * PEER_TOPK:(hd + 1) * PEER_TOPK, :] = e1 * N_KEYS + e2
        g_ref[hd * PEER_TOPK:(hd + 1) * PEER_TOPK, :] = ex * (1.0 / jnp.sum(ex, axis=0, keepdims=True))


def peer_route(h, w_query, sub_keys):
    S = h.shape[0]
    tt = LANES
    skb = sub_keys.reshape(PEER_HEADS * 2, N_KEYS, PEER_HALF).astype(_BF16)
    return pl.pallas_call(
        _peer_route_kernel,
        grid=(S // tt,),
        in_specs=[pl.BlockSpec((tt, D_MODEL), lambda i: (i, 0)),
                  pl.BlockSpec((D_MODEL, PEER_HEADS * 2 * PEER_HALF), lambda i: (0, 0)),
                  pl.BlockSpec((PEER_HEADS * 2, N_KEYS, PEER_HALF), lambda i: (0, 0, 0))],
        out_specs=[pl.BlockSpec((PEER_K, tt), lambda i: (0, i)),
                   pl.BlockSpec((PEER_K, tt), lambda i: (0, i))],
        out_shape=[jax.ShapeDtypeStruct((PEER_K, S), jnp.int32),
                   jax.ShapeDtypeStruct((PEER_K, S), _F32)],
        compiler_params=_cparams(("arbitrary",)),
        name="peer_route",
    )(h, w_query.astype(_BF16), skb)


def _row_copy(uv_hbm, buf, sem, e, r):
    return pltpu.make_async_copy(uv_hbm.at[e], buf.at[r], sem)


def _wait_all_rows(buf, sem):
    pltpu.make_async_copy(buf, buf, sem).wait()


def _dot_f32_rows(x, w):
    T = x.shape[0]
    hi = x.astype(w.dtype)
    r1 = x - hi.astype(_F32)
    mid = r1.astype(w.dtype)
    lo = (r1 - mid.astype(_F32)).astype(w.dtype)
    y = jnp.dot(jnp.concatenate([hi, mid, lo], axis=0), w, preferred_element_type=_F32)
    return y[0:T] + y[T:2 * T] + y[2 * T:3 * T]


def _peer_batch(src, issue, h_ref, o_ref, t0, g, gsum, gexp, g2, b2, h3_ref, c_ref, r3_ref):
    wide = PEER_K * D_CHUNKS
    diag = (lax.broadcasted_iota(jnp.int32, (D_CHUNKS, wide), 1) & (D_CHUNKS - 1)) == \
        lax.broadcasted_iota(jnp.int32, (D_CHUNKS, wide), 0)
    tok = lax.broadcasted_iota(jnp.int32, (PEER_TB, wide), 0)
    n_it = PEER_TB // PEER_TPB
    for t in range(PEER_TB):
        h3_ref[t] = jnp.concatenate([h_ref[t0 + t:t0 + t + 1, s * LANES:(s + 1) * LANES]
                                     for s in range(D_CHUNKS)], axis=0)

    def u_side(it, parts):
        issue(it)
        for j in range(PEER_TPB):
            t = it * PEER_TPB + j
            ut = src[pl.ds(t * PEER_K, PEER_K), 0:D_CHUNKS, :].reshape(wide, LANES)
            y = lax.dot_general(h3_ref[t].astype(_BF16), ut, _NT, preferred_element_type=_F32)
            part = jnp.sum(jnp.where(diag, y, 0.0), axis=0, keepdims=True)
            parts = jnp.where(tok == t, part, parts)
        return parts

    parts = lax.fori_loop(0, n_it, u_side, jnp.zeros((PEER_TB, wide), _F32))
    a = _dot_f32_rows(parts, gsum)
    c = (jax.nn.gelu(a) * g).astype(_BF16)
    c_ref[...] = jnp.dot(c, gexp, preferred_element_type=_F32)

    def v_side(it, carry):
        issue(n_it + it)
        for j in range(PEER_TPB):
            t = it * PEER_TPB + j
            lt = jnp.where(diag, c_ref[pl.ds(t, 1), :], 0.0).astype(_BF16)
            vt = src[pl.ds(t * PEER_K, PEER_K), D_CHUNKS:2 * D_CHUNKS, :].reshape(wide, LANES)
            r3_ref[t] = ALPHA * h3_ref[t] + jnp.dot(lt, vt, preferred_element_type=_F32)
        return carry

    lax.fori_loop(0, n_it, v_side, 0)
    for t in range(PEER_TB):
        r = r3_ref[t]
        mu = jnp.mean(r, axis=(0, 1), keepdims=True)
        d = r - mu
        var = jnp.mean(d * d, axis=(0, 1), keepdims=True)
        y = d * lax.rsqrt(var + LN_EPS) * g2 + b2
        for s in range(D_CHUNKS):
            o_ref[t0 + t:t0 + t + 1, s * LANES:(s + 1) * LANES] = y[s:s + 1, :]


def _peer_apply_kernel(eid_cur_ref, eid_nxt_ref, h_ref, g_ref, gs_ref, ex_ref, g2_ref, b2_ref, uv_hbm, o_ref,
                       buf_a, buf_b, sem, h3_ref, c_ref, r3_ref):
    i = pl.program_id(0)
    n = pl.num_programs(0)
    n_parts = 2 * (PEER_TB // PEER_TPB)
    per_part = PEER_ROWS // n_parts

    def issuer(eid_ref, e0, buf, s):
        def issue(part):
            base = pl.multiple_of(part * per_part, per_part)
            rows = buf.at[pl.ds(base, per_part)]
            for j in range(per_part):
                _row_copy(uv_hbm, rows, s, eid_ref[0, 0, e0 + base + j], j).start(priority=j % 2)
        return issue

    @pl.when(i == 0)
    def _():
        first = issuer(eid_cur_ref, 0, buf_a, sem.at[0])

        def body(part, carry):
            first(part)
            return carry
        lax.fori_loop(0, n_parts, body, 0)

    _wait_all_rows(buf_a, sem.at[0])
    _peer_batch(buf_a, issuer(eid_cur_ref, PEER_ROWS, buf_b, sem.at[1]), h_ref, o_ref, 0, g_ref[0:PEER_TB, :],
                gs_ref[...], ex_ref[...], g2_ref[...], b2_ref[...], h3_ref, c_ref, r3_ref)

    _wait_all_rows(buf_b, sem.at[1])
    _peer_batch(buf_b, issuer(eid_nxt_ref, 0, buf_a, sem.at[0]), h_ref, o_ref, PEER_TB,
                g_ref[PEER_TB:2 * PEER_TB, :], gs_ref[...], ex_ref[...], g2_ref[...], b2_ref[...],
                h3_ref, c_ref, r3_ref)

    @pl.when(i == n - 1)
    def _():
        _wait_all_rows(buf_a, sem.at[0])


def pack_expert_rows(expert_u, expert_v):
    E = expert_u.shape[0]
    tile = lambda w: w.astype(_BF16).reshape(E, D_CHUNKS, LANES)
    return jnp.concatenate([tile(expert_u), tile(expert_v)], axis=1)


def peer_apply(h, eid, gates, uvp, ln_g, ln_b):
    S = h.shape[0]
    npair = S // (2 * PEER_TB)
    eid3 = eid.reshape(npair, 1, 2 * PEER_ROWS)
    wide = PEER_K * D_CHUNKS
    expand = (np.arange(wide)[None, :] // D_CHUNKS) == np.arange(PEER_K)[:, None]
    gbuf = pltpu.VMEM((PEER_ROWS, 2 * D_CHUNKS, LANES), uvp.dtype)
    return pl.pallas_call(
        _peer_apply_kernel,
        grid=(npair,),
        in_specs=[
            pl.BlockSpec((1, 1, 2 * PEER_ROWS), lambda i: (i, 0, 0), memory_space=pltpu.SMEM),
            pl.BlockSpec((1, 1, 2 * PEER_ROWS), lambda i: (jnp.minimum(i + 1, npair - 1), 0, 0),
                         memory_space=pltpu.SMEM),
            pl.BlockSpec((2 * PEER_TB, D_MODEL), lambda i: (i, 0)),
            pl.BlockSpec((2 * PEER_TB, PEER_K), lambda i: (i, 0)),
            pl.BlockSpec((wide, PEER_K), lambda i: (0, 0)),
            pl.BlockSpec((PEER_K, wide), lambda i: (0, 0)),
            pl.BlockSpec((D_CHUNKS, LANES), lambda i: (0, 0)),
            pl.BlockSpec((D_CHUNKS, LANES), lambda i: (0, 0)),
            pl.BlockSpec(memory_space=pl.ANY),
        ],
        out_specs=pl.BlockSpec((2 * PEER_TB, D_MODEL), lambda i: (i, 0)),
        out_shape=jax.ShapeDtypeStruct((S, D_MODEL), _F32),
        scratch_shapes=[gbuf, gbuf, pltpu.SemaphoreType.DMA((2,)),
                        pltpu.VMEM((PEER_TB, D_CHUNKS, LANES), _F32),
                        pltpu.VMEM((PEER_TB, wide), _F32),
                        pltpu.VMEM((PEER_TB, D_CHUNKS, LANES), _F32)],
        compiler_params=_cparams(("arbitrary",)),
        name="peer_apply",
    )(eid3, eid3, h, gates, jnp.asarray(expand.T, _BF16), jnp.asarray(expand, _BF16),
      ln_g.reshape(D_CHUNKS, LANES), ln_b.reshape(D_CHUNKS, LANES), uvp)


def _pad_w_in(w):
    offs = np.cumsum([0, ATT_WIDTH] + [KV_WIDTH] * 6 + [N_ATT_HEADS * N_BRANCH] + [CONV_WIDTH] * 3)
    seg = lambda k: w[:, offs[k]:offs[k + 1]]
    q, kc, vc, ks, vs, kw, vw, gate, u, gb, gc = [seg(k) for k in range(11)]
    gate = jnp.pad(gate, ((0, 0), (0, GATE_PAD - gate.shape[1])))
    return jnp.concatenate([q, u, gb, gc, kc, vc, ks, vs, kw, vw, gate], axis=1)


def _layer(x2, w_in, cmp_k, cmp_v, conv_w, head_norm_g, w_out, ln1, w_query, sub_keys, expert_u, expert_v, ln2):
    S = x2.shape[0]
    G = N_KV_GROUPS
    proj, projb = in_proj(x2.astype(_BF16), _pad_w_in(w_in).astype(_BF16))

    def half_blocks(off):
        t = proj[:, off:off + KV_WIDTH].reshape(S, G, HEAD_DIM).transpose(1, 0, 2)
        return t.reshape(G, S // CMP_STRIDE, CMP_STRIDE * HEAD_DIM)

    kc = compress(half_blocks(OFF_KC), *cmp_k)
    vc = compress(half_blocks(OFF_VC), *cmp_v)
    o_att = nsa(proj, projb, kc, vc)
    h = mix_out(x2, o_att, proj, conv_w, head_norm_g, w_out, *ln1)
    eid_t, g_t = peer_route(h, w_query, sub_keys)
    return peer_apply(h, eid_t.T, g_t.T, pack_expert_rows(expert_u, expert_v), *ln2)


def kernel(x, w_in, cmp_k_pos, cmp_k_w1, cmp_k_b1, cmp_k_w2, cmp_v_pos, cmp_v_w1, cmp_v_b1, cmp_v_w2, conv_w, head_norm_g, w_out, ln1_g, ln1_b, w_query, sub_keys, expert_u, expert_v, ln2_g, ln2_b):
    B, S, D = x.shape
    outs = []
    for b in range(B):
        xb = x[b]
        for l in range(w_in.shape[0]):
            xb = _layer(xb, w_in[l],
                        (cmp_k_pos[l], cmp_k_w1[l], cmp_k_b1[l], cmp_k_w2[l]),
                        (cmp_v_pos[l], cmp_v_w1[l], cmp_v_b1[l], cmp_v_w2[l]),
                        conv_w[l], head_norm_g[l], w_out[l], (ln1_g[l], ln1_b[l]),
                        w_query[l], sub_keys[l], expert_u[l], expert_v[l], (ln2_g[l], ln2_b[l]))
        outs.append(xb)
    return jnp.stack(outs, axis=0)
```

```python
import functools

import numpy as np
import jax
import jax.numpy as jnp
from jax import lax
from jax.experimental import pallas as pl
from jax.experimental.pallas import tpu as pltpu

D_MODEL = 2048
HEAD_DIM = 128
N_ATT_HEADS = 8
N_KV_GROUPS = 2
HEADS_PER_GROUP = N_ATT_HEADS // N_KV_GROUPS
ATT_WIDTH = N_ATT_HEADS * HEAD_DIM
KV_WIDTH = N_KV_GROUPS * HEAD_DIM
CONV_WIDTH = 1024
MIX_WIDTH = ATT_WIDTH + CONV_WIDTH
N_MIX_GROUPS = MIX_WIDTH // HEAD_DIM
N_BRANCH = 3
CMP_BLOCK = 32
CMP_STRIDE = 16
CMP_HIDDEN = 512
SLC_BLOCK = 64
N_SELECT = 16
WINDOW = 512
Q_BLOCK = 128
CONV_K = 3
PEER_HEADS = 8
N_KEYS = 128
PEER_HALF = 128
PEER_TOPK = 16
PEER_K = PEER_HEADS * PEER_TOPK
DEPTH = 1
ALPHA = (2.0 * DEPTH) ** 0.25
LN_EPS = 1e-5
RMS_EPS = 1e-6
NEG_INF = -1e30
FORCE_BONUS = 1e4

LANES = 128
SUBLANES = 8
VMEM_LIMIT_BYTES = 56 * 1024 * 1024

GATE_PAD = LANES
OFF_Q = 0
OFF_U = OFF_Q + ATT_WIDTH
OFF_GB = OFF_U + CONV_WIDTH
OFF_GC = OFF_GB + CONV_WIDTH
OFF_KC = OFF_GC + CONV_WIDTH
OFF_VC = OFF_KC + KV_WIDTH
OFF_KS = OFF_VC + KV_WIDTH
OFF_VS = OFF_KS + KV_WIDTH
OFF_KW = OFF_VS + KV_WIDTH
OFF_VW = OFF_KW + KV_WIDTH
OFF_GATE = OFF_VW + KV_WIDTH
PROJ_W = OFF_GATE + GATE_PAD

SLC_TILE = 512
SLC_PER_TILE = SLC_TILE // SLC_BLOCK
SLC_BLOCK_LOG2 = SLC_BLOCK.bit_length() - 1
CMP_STRIDE_LOG2 = CMP_STRIDE.bit_length() - 1
assert 1 << SLC_BLOCK_LOG2 == SLC_BLOCK and 1 << CMP_STRIDE_LOG2 == CMP_STRIDE
WIN_KEYS = WINDOW + Q_BLOCK
WIN_BLOCKS = WIN_KEYS // Q_BLOCK

PEER_TB = 16
PEER_ROWS = PEER_TB * PEER_K
D_CHUNKS = D_MODEL // LANES
PEER_TPB = 2
assert D_CHUNKS & (D_CHUNKS - 1) == 0 and PEER_TB % PEER_TPB == 0

_BF16 = jnp.bfloat16
_F32 = jnp.float32
_NT = (((1,), (1,)), ((), ()))


def _cparams(sem):
    return pltpu.CompilerParams(dimension_semantics=sem, vmem_limit_bytes=VMEM_LIMIT_BYTES)


def _matmul_kernel(x_ref, w_ref, o_ref, ob_ref):
    acc = jnp.dot(x_ref[...], w_ref[...], preferred_element_type=_F32)
    o_ref[...] = acc
    ob_ref[...] = acc.astype(ob_ref.dtype)


def in_proj(xb, wb):
    S, K = xb.shape
    N = wb.shape[1]
    tm = min(512, S)
    tn = N // 5
    return pl.pallas_call(
        _matmul_kernel,
        grid=(N // tn, S // tm),
        in_specs=[pl.BlockSpec((tm, K), lambda j, i: (i, 0)),
                  pl.BlockSpec((K, tn), lambda j, i: (0, j))],
        out_specs=[pl.BlockSpec((tm, tn), lambda j, i: (i, j)),
                   pl.BlockSpec((tm, tn), lambda j, i: (i, j))],
        out_shape=[jax.ShapeDtypeStruct((S, N), _F32), jax.ShapeDtypeStruct((S, N), wb.dtype)],
        compiler_params=_cparams(("arbitrary", "arbitrary")),
        name="in_proj",
    )(xb, wb)


def _compress_kernel(hb_ref, pos_ref, w1_ref, b1_ref, w2_ref, o_ref):
    nh = hb_ref.shape[1]
    half = hb_ref.shape[2]
    hb = hb_ref[0]
    top = (hb + pos_ref[:, :half]).astype(_BF16)
    bot = (hb + pos_ref[:, half:]).astype(_BF16)
    a = jnp.dot(top, w1_ref[:half, :], preferred_element_type=_F32)
    b = jnp.dot(bot, w1_ref[half:, :], preferred_element_type=_F32)
    hidden = a + pltpu.roll(b, nh - 1, 0) + b1_ref[...]
    act = jax.nn.gelu(hidden).astype(_BF16)
    o_ref[0] = jnp.dot(act, w2_ref[...], preferred_element_type=_F32)


def compress(hb, pos, w1, b1, w2):
    G, nh, half = hb.shape
    posflat = pos.reshape(1, CMP_BLOCK * HEAD_DIM)
    return pl.pallas_call(
        _compress_kernel,
        grid=(G,),
        in_specs=[pl.BlockSpec((1, nh, half), lambda g: (g, 0, 0)),
                  pl.BlockSpec((1, 2 * half), lambda g: (0, 0)),
                  pl.BlockSpec((2 * half, CMP_HIDDEN), lambda g: (0, 0)),
                  pl.BlockSpec((1, CMP_HIDDEN), lambda g: (0, 0)),
                  pl.BlockSpec((CMP_HIDDEN, HEAD_DIM), lambda g: (0, 0))],
        out_specs=pl.BlockSpec((1, nh, HEAD_DIM), lambda g: (g, 0, 0)),
        out_shape=jax.ShapeDtypeStruct((G, nh, HEAD_DIM), _F32),
        compiler_params=_cparams(("arbitrary",)),
        name="compress",
    )(hb, posflat, w1.astype(_BF16), b1.reshape(1, CMP_HIDDEN), w2.astype(_BF16))


def _nsa_kernel(q_ref, kc_ref, vct_ref, ks_ref, vst_ref, kw_ref, vwt_ref, gl_ref, slope_ref, cb_ref, rb_ref,
                ovt_ref, o_ref, sel_ref, m_ref, l_ref, acc_ref, flag_ref, *, n_sel):
    nb = pl.program_id(1)
    s0 = nb * Q_BLOCK
    R = HEADS_PER_GROUP
    L = R * Q_BLOCK
    nh = kc_ref.shape[1]
    n_slc = ovt_ref.shape[0]
    scale = HEAD_DIM ** -0.5

    qs = jnp.concatenate([q_ref[:, r * HEAD_DIM:(r + 1) * HEAD_DIM] for r in range(R)], axis=0)
    qs = (qs * scale).astype(_BF16)
    slope = slope_ref[0]
    lane = lax.broadcasted_iota(jnp.int32, (1, L), 1)
    t_i = s0 + (lane & (Q_BLOCK - 1))

    sc = lax.dot_general(kc_ref[0], qs, _NT, preferred_element_type=_F32)
    n_io = lax.broadcasted_iota(jnp.int32, (nh, L), 0)
    valid = n_io <= ((t_i - (CMP_BLOCK - 1)) >> CMP_STRIDE_LOG2)
    s = jnp.where(valid, sc + cb_ref[0], NEG_INF)
    m = jnp.max(s, axis=0, keepdims=True)
    e = jnp.where(valid, jnp.exp(s - m), 0.0)
    l = jnp.sum(e, axis=0, keepdims=True)
    p = e * (1.0 / jnp.maximum(l, 1e-30))
    o_cmp = jnp.dot(vct_ref[0], p.astype(_BF16), preferred_element_type=_F32)

    ps = p[:, 0:Q_BLOCK]
    for r in range(1, R):
        ps = ps + p[:, r * Q_BLOCK:(r + 1) * Q_BLOCK]
    hi = ps.astype(_BF16)
    r1 = ps - hi.astype(_F32)
    mid = r1.astype(_BF16)
    lo = (r1 - mid.astype(_F32)).astype(_BF16)
    ovt = ovt_ref[...]
    imp = (jnp.dot(ovt, hi, preferred_element_type=_F32) + jnp.dot(ovt, mid, preferred_element_type=_F32)
           + jnp.dot(ovt, lo, preferred_element_type=_F32))

    j_io = lax.broadcasted_iota(jnp.int32, (n_slc, Q_BLOCK), 0)
    tq = s0 + lax.broadcasted_iota(jnp.int32, (1, Q_BLOCK), 1)
    cur = tq >> SLC_BLOCK_LOG2
    valid_s = (j_io * SLC_BLOCK) <= tq
    forced = (j_io == 0) | (j_io == cur) | (j_io == cur - 1)
    score = jnp.where(valid_s, imp + jnp.where(forced, FORCE_BONUS, 0.0), -1.0)

    def pick(_, carry):
        work, sel = carry
        mx = jnp.max(work, axis=0, keepdims=True)
        first = jnp.min(jnp.where(work == mx, j_io, n_slc), axis=0, keepdims=True)
        hit = j_io == first
        return jnp.where(hit, -jnp.inf, work), jnp.where(hit, 1.0, sel)

    _, sel = lax.fori_loop(0, n_sel, pick, (score, jnp.zeros((n_slc, Q_BLOCK), _F32)))
    sel_ref[...] = sel
    for k in range(n_slc // SLC_PER_TILE):
        flag_ref[k] = (jnp.max(sel[k * SLC_PER_TILE:(k + 1) * SLC_PER_TILE, :]) > 0.0).astype(jnp.int32)

    m_ref[...] = jnp.full((1, L), NEG_INF, _F32)
    l_ref[...] = jnp.zeros((1, L), _F32)
    acc_ref[...] = jnp.zeros((HEAD_DIM, L), _F32)
    key_io = lax.broadcasted_iota(jnp.int32, (SLC_TILE, L), 0)
    last = (s0 + Q_BLOCK - 1) // SLC_TILE

    def tile_update(kt, causal):
        selt = sel_ref[pl.ds(pl.multiple_of(kt * SLC_PER_TILE, SLC_PER_TILE), SLC_PER_TILE), :]
        sx = jnp.broadcast_to(selt[:, None, :], (SLC_PER_TILE, SLC_BLOCK, Q_BLOCK)).reshape(SLC_TILE, Q_BLOCK)
        ok = jnp.concatenate([sx] * R, axis=1) > 0.0
        if causal:
            ok = ok & (key_io <= t_i - kt * SLC_TILE)
        sr = lax.dot_general(ks_ref[0, kt], qs, _NT, preferred_element_type=_F32)
        st = jnp.where(ok, sr + rb_ref[0, 0:SLC_TILE, :], NEG_INF)
        ck = slope * (kt * SLC_TILE).astype(_F32)
        m_old = m_ref[...]
        m_new = jnp.maximum(m_old, jnp.max(st, axis=0, keepdims=True) + ck)
        alpha = jnp.exp(m_old - m_new)
        et = jnp.exp(st - (m_new - ck))
        l_ref[...] = alpha * l_ref[...] + jnp.sum(et, axis=0, keepdims=True)
        acc_ref[...] = alpha * acc_ref[...] + jnp.dot(vst_ref[0, kt], et.astype(_BF16),
                                                      preferred_element_type=_F32)
        m_ref[...] = m_new

    def slc_tile(kt, carry):
        @pl.when(flag_ref[kt] > 0)
        def _():
            tile_update(kt, False)
        return carry

    lax.fori_loop(0, last, slc_tile, 0)
    tile_update(last, True)
    o_slc = acc_ref[...] * (1.0 / l_ref[...])

    start = pl.multiple_of(jnp.maximum(s0 - WINDOW, 0), Q_BLOCK)
    sw = lax.dot_general(kw_ref[0, pl.ds(start, WIN_KEYS), :], qs, _NT, preferred_element_type=_F32)
    row_io = lax.broadcasted_iota(jnp.int32, (WIN_KEYS, L), 0)
    rel = t_i - start
    okw = (row_io <= rel) & (row_io > rel - WINDOW)
    s = jnp.where(okw, sw + rb_ref[0], NEG_INF)
    m = jnp.max(s, axis=0, keepdims=True)
    e = jnp.exp(s - m)
    pw = (e * (1.0 / jnp.sum(e, axis=0, keepdims=True))).astype(_BF16)
    b0 = start // Q_BLOCK
    o_win = jnp.dot(vwt_ref[0, b0], pw[0:Q_BLOCK], preferred_element_type=_F32)
    for i in range(1, WIN_BLOCKS):
        o_win = o_win + jnp.dot(vwt_ref[0, b0 + i], pw[i * Q_BLOCK:(i + 1) * Q_BLOCK],
                                preferred_element_type=_F32)

    sg = jax.nn.sigmoid(gl_ref[0, 0])
    o = sg[0:1] * o_cmp + sg[1:2] * o_slc + sg[2:3] * o_win
    for r in range(R):
        o_ref[:, r * HEAD_DIM:(r + 1) * HEAD_DIM] = o[:, r * Q_BLOCK:(r + 1) * Q_BLOCK].T


def nsa(proj, projb, kc, vc):
    S = proj.shape[0]
    G, R = N_KV_GROUPS, HEADS_PER_GROUP
    nh = S // CMP_STRIDE
    n_slc = S // SLC_BLOCK
    n_qb = S // Q_BLOCK
    n_kt = S // SLC_TILE
    n_sel = min(N_SELECT, n_slc)
    assert n_sel >= 3 and S >= WIN_KEYS
    L = R * Q_BLOCK

    def grp(off):
        return projb[:, off:off + KV_WIDTH].reshape(S, G, HEAD_DIM).transpose(1, 0, 2)

    kcb = kc.astype(_BF16)
    vct = vc.transpose(0, 2, 1).astype(_BF16)
    ks3 = grp(OFF_KS).reshape(G, n_kt, SLC_TILE, HEAD_DIM)
    vst3 = grp(OFF_VS).reshape(G, n_kt, SLC_TILE, HEAD_DIM).transpose(0, 1, 3, 2)
    kw = grp(OFF_KW)
    vwt3 = grp(OFF_VW).reshape(G, n_qb, Q_BLOCK, HEAD_DIM).transpose(0, 1, 3, 2)
    gl = proj[:, OFF_GATE:OFF_GATE + N_ATT_HEADS * N_BRANCH].reshape(n_qb, Q_BLOCK, G, R, N_BRANCH)
    gl = gl.transpose(2, 0, 4, 3, 1).reshape(G, n_qb, N_BRANCH, L)
    head = np.arange(N_ATT_HEADS, dtype=np.float64).reshape(G, R)
    slopes = np.repeat(2.0 ** (-8.0 * (head + 1) / N_ATT_HEADS), Q_BLOCK, axis=1).reshape(G, 1, L)
    slopes = jnp.asarray(slopes, _F32)
    cpos = jnp.arange(nh, dtype=_F32) * CMP_STRIDE + (CMP_BLOCK - 1) / 2.0
    cmp_bias = slopes * cpos[None, :, None]
    row_bias = slopes * jnp.arange(WIN_KEYS, dtype=_F32)[None, :, None]
    ci = np.arange(nh)[None, :] * CMP_STRIDE
    sj = np.arange(n_slc)[:, None] * SLC_BLOCK
    ovt = ((ci < sj + SLC_BLOCK) & (ci + CMP_BLOCK > sj) & (np.arange(nh)[None, :] < nh - 1))

    return pl.pallas_call(
        functools.partial(_nsa_kernel, n_sel=n_sel),
        grid=(G, n_qb),
        in_specs=[
            pl.BlockSpec((Q_BLOCK, L), lambda g, i: (i, g)),
            pl.BlockSpec((1, nh, HEAD_DIM), lambda g, i: (g, 0, 0)),
            pl.BlockSpec((1, HEAD_DIM, nh), lambda g, i: (g, 0, 0)),
            pl.BlockSpec((1, n_kt, SLC_TILE, HEAD_DIM), lambda g, i: (g, 0, 0, 0)),
            pl.BlockSpec((1, n_kt, HEAD_DIM, SLC_TILE), lambda g, i: (g, 0, 0, 0)),
            pl.BlockSpec((1, S, HEAD_DIM), lambda g, i: (g, 0, 0)),
            pl.BlockSpec((1, n_qb, HEAD_DIM, Q_BLOCK), lambda g, i: (g, 0, 0, 0)),
            pl.BlockSpec((1, 1, N_BRANCH, L), lambda g, i: (g, i, 0, 0)),
            pl.BlockSpec((1, 1, L), lambda g, i: (g, 0, 0)),
            pl.BlockSpec((1, nh, L), lambda g, i: (g, 0, 0)),
            pl.BlockSpec((1, WIN_KEYS, L), lambda g, i: (g, 0, 0)),
            pl.BlockSpec((n_slc, nh), lambda g, i: (0, 0)),
        ],
        out_specs=pl.BlockSpec((Q_BLOCK, L), lambda g, i: (i, g)),
        out_shape=jax.ShapeDtypeStruct((S, ATT_WIDTH), _F32),
        scratch_shapes=[pltpu.VMEM((n_slc, Q_BLOCK), _F32),
                        pltpu.VMEM((1, L), _F32),
                        pltpu.VMEM((1, L), _F32),
                        pltpu.VMEM((HEAD_DIM, L), _F32),
                        pltpu.SMEM((n_kt,), jnp.int32)],
        compiler_params=_cparams(("arbitrary", "arbitrary")),
        name="nsa",
    )(proj, kcb, vct, ks3, vst3, kw, vwt3, gl, slopes, cmp_bias, row_bias, jnp.asarray(ovt, _BF16))


def _layernorm_rows(r, g, b):
    mu = jnp.mean(r, axis=-1, keepdims=True)
    c = r - mu
    var = jnp.mean(c * c, axis=-1, keepdims=True)
    return c * lax.rsqrt(var + LN_EPS) * g + b


def _mix_out_kernel(x_ref, att_ref, u_ref, gb_ref, gc_ref, uh_ref, gch_ref, cw_ref, hg_ref, wo_ref,
                    g1_ref, b1_ref, h_ref):
    i = pl.program_id(0)
    tm = x_ref.shape[0]
    z = gc_ref[...] * u_ref[...]
    zh = jnp.where(i > 0, gch_ref[...] * uh_ref[...], 0.0)
    zz = jnp.concatenate([zh, z], axis=0)
    z1 = pltpu.roll(zz, 1, 0)[SUBLANES:]
    z2 = pltpu.roll(zz, 2, 0)[SUBLANES:]
    y = cw_ref[0:1, :] * z2 + cw_ref[1:2, :] * z1 + cw_ref[2:3, :] * z
    o_conv = gb_ref[...] * y
    mix = jnp.concatenate([att_ref[...], o_conv], axis=1)
    parts = []
    for k in range(N_MIX_GROUPS):
        blk = mix[:, k * HEAD_DIM:(k + 1) * HEAD_DIM]
        ms = jnp.mean(blk * blk, axis=-1, keepdims=True)
        parts.append(blk * lax.rsqrt(ms + RMS_EPS))
    mixn = (jnp.concatenate(parts, axis=1) * hg_ref[...]).astype(_BF16)
    r = ALPHA * x_ref[...] + jnp.dot(mixn, wo_ref[...], preferred_element_type=_F32)
    h_ref[...] = _layernorm_rows(r, g1_ref[...], b1_ref[...])


def mix_out(x2, o_att, proj, conv_w, head_norm_g, w_out, ln_g, ln_b):
    S = x2.shape[0]
    tm = min(256, S)
    hb = tm // SUBLANES
    cblk = lambda off: off // CONV_WIDTH
    halo = lambda off: pl.BlockSpec((SUBLANES, CONV_WIDTH), lambda i: (jnp.maximum(i * hb - 1, 0), cblk(off)))
    tile = lambda off: pl.BlockSpec((tm, CONV_WIDTH), lambda i: (i, cblk(off)))
    full = lambda shape: pl.BlockSpec(shape, lambda i: (0, 0))
    cw = jnp.zeros((SUBLANES, CONV_WIDTH), _F32).at[:CONV_K].set(conv_w)
    return pl.pallas_call(
        _mix_out_kernel,
        grid=(S // tm,),
        in_specs=[pl.BlockSpec((tm, D_MODEL), lambda i: (i, 0)),
                  pl.BlockSpec((tm, ATT_WIDTH), lambda i: (i, 0)),
                  tile(OFF_U), tile(OFF_GB), tile(OFF_GC), halo(OFF_U), halo(OFF_GC),
                  full((SUBLANES, CONV_WIDTH)), full((1, MIX_WIDTH)), full((MIX_WIDTH, D_MODEL)),
                  full((1, D_MODEL)), full((1, D_MODEL))],
        out_specs=pl.BlockSpec((tm, D_MODEL), lambda i: (i, 0)),
        out_shape=jax.ShapeDtypeStruct((S, D_MODEL), _F32),
        compiler_params=_cparams(("arbitrary",)),
        name="mix_out",
    )(x2, o_att, proj, proj, proj, proj, proj, cw, head_norm_g.reshape(1, MIX_WIDTH), w_out.astype(_BF16),
      ln_g.reshape(1, D_MODEL), ln_b.reshape(1, D_MODEL))


def _topk_rows(x, k):
    R, L = x.shape
    io = lax.broadcasted_iota(jnp.int32, (R, L), 0)
    ko = lax.broadcasted_iota(jnp.int32, (k, L), 0)
    vals = jnp.zeros((k, L), _F32)
    idxs = jnp.zeros((k, L), jnp.int32)
    for i in range(k):
        mx = jnp.max(x, axis=0, keepdims=True)
        first = jnp.min(jnp.where(x == mx, io, R), axis=0, keepdims=True)
        x = jnp.where(io == first, -jnp.inf, x)
        vals = jnp.where(ko == i, mx, vals)
        idxs = jnp.where(ko == i, first, idxs)
    return vals, idxs


def _take_rows(table, idx):
    out = jnp.zeros(idx.shape, table.dtype)
    for a in range(table.shape[0]):
        out = jnp.where(idx == a, table[a:a + 1, :], out)
    return out


def _peer_route_kernel(h_ref, wq_ref, sk_ref, eid_ref, g_ref):
    qp = jnp.dot(h_ref[...].astype(_BF16), wq_ref[...], preferred_element_type=_F32)
    for hd in range(PEER_HEADS):
        tops = []
        for c in range(2):
            j = hd * 2 + c
            qhc = qp[:, j * PEER_HALF:(j + 1) * PEER_HALF].astype(_BF16)
            st = lax.dot_general(sk_ref[j], qhc, _NT, preferred_element_type=_F32)
            tops.append(_topk_rows(st, PEER_TOPK))
        (v1, i1), (v2, i2) = tops
        nb = [PEER_TOPK // (a + 1) for a in range(PEER_TOPK)]
        tt = i1.shape[1]
        pad = -sum(nb) % SUBLANES
        cand = jnp.concatenate([v1[a:a + 1, :] + v2[0:nb[a], :] for a in range(PEER_TOPK)]
                               + [jnp.full((pad, tt), -jnp.inf, _F32)], axis=0)
        bs, bi = _topk_rows(cand, PEER_TOPK)
        zpad = [jnp.zeros((pad, tt), jnp.int32)]
        t1 = jnp.concatenate([jnp.broadcast_to(i1[a:a + 1, :], (nb[a], tt)) for a in range(PEER_TOPK)] + zpad,
                             axis=0)
        t2 = jnp.concatenate([i2[0:nb[a], :] for a in range(PEER_TOPK)] + zpad, axis=0)
        e1 = _take_rows(t1, bi)
        e2 = _take_rows(t2, bi)
        ex = jnp.exp(bs - jnp.max(bs, axis=0, keepdims=True))
        eid_ref[hd * PEER_TOPK:(hd + 1) * PEER_TOPK, :] = e1 * N_KEYS + e2
        g_ref[hd * PEER_TOPK:(hd + 1) * PEER_TOPK, :] = ex * (1.0 / jnp.sum(ex, axis=0, keepdims=True))


def peer_route(h, w_query, sub_keys):
    S = h.shape[0]
    tt = LANES
    skb = sub_keys.reshape(PEER_HEADS * 2, N_KEYS, PEER_HALF).astype(_BF16)
    return pl.pallas_call(
        _peer_route_kernel,
        grid=(S // tt,),
        in_specs=[pl.BlockSpec((tt, D_MODEL), lambda i: (i, 0)),
                  pl.BlockSpec((D_MODEL, PEER_HEADS * 2 * PEER_HALF), lambda i: (0, 0)),
                  pl.BlockSpec((PEER_HEADS * 2, N_KEYS, PEER_HALF), lambda i: (0, 0, 0))],
        out_specs=[pl.BlockSpec((PEER_K, tt), lambda i: (0, i)),
                   pl.BlockSpec((PEER_K, tt), lambda i: (0, i))],
        out_shape=[jax.ShapeDtypeStruct((PEER_K, S), jnp.int32),
                   jax.ShapeDtypeStruct((PEER_K, S), _F32)],
        compiler_params=_cparams(("arbitrary",)),
        name="peer_route",
    )(h, w_query.astype(_BF16), skb)


def _row_copy(uv_hbm, buf, sem, e, r):
    return pltpu.make_async_copy(uv_hbm.at[e], buf.at[r], sem)


def _wait_all_rows(buf, sem):
    pltpu.make_async_copy(buf, buf, sem).wait()


def _dot_f32_rows(x, w):
    T = x.shape[0]
    hi = x.astype(w.dtype)
    r1 = x - hi.astype(_F32)
    mid = r1.astype(w.dtype)
    lo = (r1 - mid.astype(_F32)).astype(w.dtype)
    y = jnp.dot(jnp.concatenate([hi, mid, lo], axis=0), w, preferred_element_type=_F32)
    return y[0:T] + y[T:2 * T] + y[2 * T:3 * T]


def _peer_batch(src, issue, h_ref, o_ref, t0, g, gsum, gexp, g2, b2, h3_ref, c_ref, r3_ref):
    wide = PEER_K * D_CHUNKS
    diag = (lax.broadcasted_iota(jnp.int32, (D_CHUNKS, wide), 1) & (D_CHUNKS - 1)) == \
        lax.broadcasted_iota(jnp.int32, (D_CHUNKS, wide), 0)
    n_it = PEER_TB // PEER_TPB
    for t in range(PEER_TB):
        h3_ref[t] = jnp.concatenate([h_ref[t0 + t:t0 + t + 1, s * LANES:(s + 1) * LANES]
                                     for s in range(D_CHUNKS)], axis=0)

    def u_side(it, carry):
        issue(it)
        for j in range(PEER_TPB):
            t = it * PEER_TPB + j
            ut = src[pl.ds(t * PEER_K, PEER_K), 0:D_CHUNKS, :].reshape(wide, LANES)
            y = lax.dot_general(h3_ref[t].astype(_BF16), ut, _NT, preferred_element_type=_F32)
            c_ref[pl.ds(t, 1), :] = jnp.sum(jnp.where(diag, y, 0.0), axis=0, keepdims=True)
        return carry

    lax.fori_loop(0, n_it, u_side, 0)
    a = _dot_f32_rows(c_ref[...], gsum)
    c = (jax.nn.gelu(a) * g).astype(_BF16)
    c_ref[...] = jnp.dot(c, gexp, preferred_element_type=_F32)

    def v_side(it, carry):
        issue(n_it + it)
        for j in range(PEER_TPB):
            t = it * PEER_TPB + j
            lt = jnp.where(diag, c_ref[pl.ds(t, 1), :], 0.0).astype(_BF16)
            vt = src[pl.ds(t * PEER_K, PEER_K), D_CHUNKS:2 * D_CHUNKS, :].reshape(wide, LANES)
            r3_ref[t] = ALPHA * h3_ref[t] + jnp.dot(lt, vt, preferred_element_type=_F32)
        return carry

    lax.fori_loop(0, n_it, v_side, 0)
    for t in range(PEER_TB):
        r = r3_ref[t]
        mu = jnp.mean(r, axis=(0, 1), keepdims=True)
        d = r - mu
        var = jnp.mean(d * d, axis=(0, 1), keepdims=True)
        y = d * lax.rsqrt(var + LN_EPS) * g2 + b2
        for s in range(D_CHUNKS):
            o_ref[t0 + t:t0 + t + 1, s * LANES:(s + 1) * LANES] = y[s:s + 1, :]


def _peer_apply_kernel(eid_cur_ref, eid_nxt_ref, h_ref, g_ref, gs_ref, ex_ref, g2_ref, b2_ref, uv_hbm, o_ref,
                       buf_a, buf_b, sem, h3_ref, c_ref, r3_ref):
    i = pl.program_id(0)
    n = pl.num_programs(0)
    n_parts = 2 * (PEER_TB // PEER_TPB)
    per_part = PEER_ROWS // n_parts

    def issuer(eid_ref, e0, buf, s):
        def issue(part):
            base = pl.multiple_of(part * per_part, per_part)
            rows = buf.at[pl.ds(base, per_part)]
            for j in range(per_part):
                _row_copy(uv_hbm, rows, s, eid_ref[0, 0, e0 + base + j], j).start(priority=j % 2)
        return issue

    @pl.when(i == 0)
    def _():
        first = issuer(eid_cur_ref, 0, buf_a, sem.at[0])

        def body(part, carry):
            first(part)
            return carry
        lax.fori_loop(0, n_parts, body, 0)

    _wait_all_rows(buf_a, sem.at[0])
    _peer_batch(buf_a, issuer(eid_cur_ref, PEER_ROWS, buf_b, sem.at[1]), h_ref, o_ref, 0, g_ref[0:PEER_TB, :],
                gs_ref[...], ex_ref[...], g2_ref[...], b2_ref[...], h3_ref, c_ref, r3_ref)

    _wait_all_rows(buf_b, sem.at[1])
    _peer_batch(buf_b, issuer(eid_nxt_ref, 0, buf_a, sem.at[0]), h_ref, o_ref, PEER_TB,
                g_ref[PEER_TB:2 * PEER_TB, :], gs_ref[...], ex_ref[...], g2_ref[...], b2_ref[...],
                h3_ref, c_ref, r3_ref)

    @pl.when(i == n - 1)
    def _():
        _wait_all_rows(buf_a, sem.at[0])


def pack_expert_rows(expert_u, expert_v):
    E = expert_u.shape[0]
    tile = lambda w: w.reshape(E, D_CHUNKS, LANES)
    return jnp.concatenate([tile(expert_u), tile(expert_v)], axis=1).astype(_BF16)


def peer_apply(h, eid, gates, uvp, ln_g, ln_b):
    S = h.shape[0]
    npair = S // (2 * PEER_TB)
    eid3 = eid.reshape(npair, 1, 2 * PEER_ROWS)
    wide = PEER_K * D_CHUNKS
    expand = (np.arange(wide)[None, :] // D_CHUNKS) == np.arange(PEER_K)[:, None]
    gbuf = pltpu.VMEM((PEER_ROWS, 2 * D_CHUNKS, LANES), uvp.dtype)
    return pl.pallas_call(
        _peer_apply_kernel,
        grid=(npair,),
        in_specs=[
            pl.BlockSpec((1, 1, 2 * PEER_ROWS), lambda i: (i, 0, 0), memory_space=pltpu.SMEM),
            pl.BlockSpec((1, 1, 2 * PEER_ROWS), lambda i: (jnp.minimum(i + 1, npair - 1), 0, 0),
                         memory_space=pltpu.SMEM),
            pl.BlockSpec((2 * PEER_TB, D_MODEL), lambda i: (i, 0)),
            pl.BlockSpec((2 * PEER_TB, PEER_K), lambda i: (i, 0)),
            pl.BlockSpec((wide, PEER_K), lambda i: (0, 0)),
            pl.BlockSpec((PEER_K, wide), lambda i: (0, 0)),
            pl.BlockSpec((D_CHUNKS, LANES), lambda i: (0, 0)),
            pl.BlockSpec((D_CHUNKS, LANES), lambda i: (0, 0)),
            pl.BlockSpec(memory_space=pl.ANY),
        ],
        out_specs=pl.BlockSpec((2 * PEER_TB, D_MODEL), lambda i: (i, 0)),
        out_shape=jax.ShapeDtypeStruct((S, D_MODEL), _F32),
        scratch_shapes=[gbuf, gbuf, pltpu.SemaphoreType.DMA((2,)),
                        pltpu.VMEM((PEER_TB, D_CHUNKS, LANES), _F32),
                        pltpu.VMEM((PEER_TB, wide), _F32),
                        pltpu.VMEM((PEER_TB, D_CHUNKS, LANES), _F32)],
        compiler_params=_cparams(("arbitrary",)),
        name="peer_apply",
    )(eid3, eid3, h, gates, jnp.asarray(expand.T, _BF16), jnp.asarray(expand, _BF16),
      ln_g.reshape(D_CHUNKS, LANES), ln_b.reshape(D_CHUNKS, LANES), uvp)


def _pad_w_in(w):
    offs = np.cumsum([0, ATT_WIDTH] + [KV_WIDTH] * 6 + [N_ATT_HEADS * N_BRANCH] + [CONV_WIDTH] * 3)
    seg = lambda k: w[:, offs[k]:offs[k + 1]]
    q, kc, vc, ks, vs, kw, vw, gate, u, gb, gc = [seg(k) for k in range(11)]
    gate = jnp.pad(gate, ((0, 0), (0, GATE_PAD - gate.shape[1])))
    return jnp.concatenate([q, u, gb, gc, kc, vc, ks, vs, kw, vw, gate], axis=1)


def _layer(x2, w_in, cmp_k, cmp_v, conv_w, head_norm_g, w_out, ln1, w_query, sub_keys, expert_u, expert_v, ln2):
    S = x2.shape[0]
    G = N_KV_GROUPS
    proj, projb = in_proj(x2.astype(_BF16), _pad_w_in(w_in).astype(_BF16))

    def half_blocks(off):
        t = proj[:, off:off + KV_WIDTH].reshape(S, G, HEAD_DIM).transpose(1, 0, 2)
        return t.reshape(G, S // CMP_STRIDE, CMP_STRIDE * HEAD_DIM)

    kc = compress(half_blocks(OFF_KC), *cmp_k)
    vc = compress(half_blocks(OFF_VC), *cmp_v)
    o_att = nsa(proj, projb, kc, vc)
    h = mix_out(x2, o_att, proj, conv_w, head_norm_g, w_out, *ln1)
    eid_t, g_t = peer_route(h, w_query, sub_keys)
    return peer_apply(h, eid_t.T, g_t.T, pack_expert_rows(expert_u, expert_v), *ln2)


def kernel(x, w_in, cmp_k_pos, cmp_k_w1, cmp_k_b1, cmp_k_w2, cmp_v_pos, cmp_v_w1, cmp_v_b1, cmp_v_w2, conv_w, head_norm_g, w_out, ln1_g, ln1_b, w_query, sub_keys, expert_u, expert_v, ln2_g, ln2_b):
    B, S, D = x.shape
    outs = []
    for b in range(B):
        xb = x[b]
        for l in range(w_in.shape[0]):
            xb = _layer(xb, w_in[l],
                        (cmp_k_pos[l], cmp_k_w1[l], cmp_k_b1[l], cmp_k_w2[l]),
                        (cmp_v_pos[l], cmp_v_w1[l], cmp_v_b1[l], cmp_v_w2[l]),
                        conv_w[l], head_norm_g[l], w_out[l], (ln1_g[l], ln1_b[l]),
                        w_query[l], sub_keys[l], expert_u[l], expert_v[l], (ln2_g[l], ln2_b[l]))
        outs.append(xb)
    return jnp.stack(outs, axis=0)
```

```python
import functools

import numpy as np
import jax
import jax.numpy as jnp
from jax import lax
from jax.experimental import pallas as pl
from jax.experimental.pallas import tpu as pltpu

D_MODEL = 2048
HEAD_DIM = 128
N_ATT_HEADS = 8
N_KV_GROUPS = 2
HEADS_PER_GROUP = N_ATT_HEADS // N_KV_GROUPS
ATT_WIDTH = N_ATT_HEADS * HEAD_DIM
KV_WIDTH = N_KV_GROUPS * HEAD_DIM
CONV_WIDTH = 1024
MIX_WIDTH = ATT_WIDTH + CONV_WIDTH
N_MIX_GROUPS = MIX_WIDTH // HEAD_DIM
N_BRANCH = 3
CMP_BLOCK = 32
CMP_STRIDE = 16
CMP_HIDDEN = 512
SLC_BLOCK = 64
N_SELECT = 16
WINDOW = 512
Q_BLOCK = 128
CONV_K = 3
PEER_HEADS = 8
N_KEYS = 128
PEER_HALF = 128
PEER_TOPK = 16
PEER_K = PEER_HEADS * PEER_TOPK
DEPTH = 1
ALPHA = (2.0 * DEPTH) ** 0.25
LN_EPS = 1e-5
RMS_EPS = 1e-6
NEG_INF = -1e30
FORCE_BONUS = 1e4

LANES = 128
SUBLANES = 8
VMEM_LIMIT_BYTES = 56 * 1024 * 1024

GATE_PAD = LANES
OFF_Q = 0
OFF_U = OFF_Q + ATT_WIDTH
OFF_GB = OFF_U + CONV_WIDTH
OFF_GC = OFF_GB + CONV_WIDTH
OFF_KC = OFF_GC + CONV_WIDTH
OFF_VC = OFF_KC + KV_WIDTH
OFF_KS = OFF_VC + KV_WIDTH
OFF_VS = OFF_KS + KV_WIDTH
OFF_KW = OFF_VS + KV_WIDTH
OFF_VW = OFF_KW + KV_WIDTH
OFF_GATE = OFF_VW + KV_WIDTH
PROJ_W = OFF_GATE + GATE_PAD

SLC_TILE = 512
SLC_PER_TILE = SLC_TILE // SLC_BLOCK
SLC_BLOCK_LOG2 = SLC_BLOCK.bit_length() - 1
CMP_STRIDE_LOG2 = CMP_STRIDE.bit_length() - 1
assert 1 << SLC_BLOCK_LOG2 == SLC_BLOCK and 1 << CMP_STRIDE_LOG2 == CMP_STRIDE
WIN_KEYS = WINDOW + Q_BLOCK
WIN_BLOCKS = WIN_KEYS // Q_BLOCK

PEER_TB = 16
PEER_ROWS = PEER_TB * PEER_K
D_CHUNKS = D_MODEL // LANES
PEER_TPB = 4
assert D_CHUNKS & (D_CHUNKS - 1) == 0 and PEER_TB % PEER_TPB == 0

_BF16 = jnp.bfloat16
_F32 = jnp.float32
_NT = (((1,), (1,)), ((), ()))


def _cparams(sem):
    return pltpu.CompilerParams(dimension_semantics=sem, vmem_limit_bytes=VMEM_LIMIT_BYTES)


def _matmul_kernel(x_ref, w_ref, o_ref, ob_ref):
    acc = jnp.dot(x_ref[...], w_ref[...], preferred_element_type=_F32)
    o_ref[...] = acc
    ob_ref[...] = acc.astype(ob_ref.dtype)


def in_proj(xb, wb):
    S, K = xb.shape
    N = wb.shape[1]
    tm = min(512, S)
    tn = N // 5
    return pl.pallas_call(
        _matmul_kernel,
        grid=(N // tn, S // tm),
        in_specs=[pl.BlockSpec((tm, K), lambda j, i: (i, 0)),
                  pl.BlockSpec((K, tn), lambda j, i: (0, j))],
        out_specs=[pl.BlockSpec((tm, tn), lambda j, i: (i, j)),
                   pl.BlockSpec((tm, tn), lambda j, i: (i, j))],
        out_shape=[jax.ShapeDtypeStruct((S, N), _F32), jax.ShapeDtypeStruct((S, N), wb.dtype)],
        compiler_params=_cparams(("arbitrary", "arbitrary")),
        name="in_proj",
    )(xb, wb)


def _compress_kernel(hb_ref, pos_ref, w1_ref, b1_ref, w2_ref, o_ref):
    nh = hb_ref.shape[1]
    half = hb_ref.shape[2]
    hb = hb_ref[0]
    top = (hb + pos_ref[:, :half]).astype(_BF16)
    bot = (hb + pos_ref[:, half:]).astype(_BF16)
    a = jnp.dot(top, w1_ref[:half, :], preferred_element_type=_F32)
    b = jnp.dot(bot, w1_ref[half:, :], preferred_element_type=_F32)
    hidden = a + pltpu.roll(b, nh - 1, 0) + b1_ref[...]
    act = jax.nn.gelu(hidden).astype(_BF16)
    o_ref[0] = jnp.dot(act, w2_ref[...], preferred_element_type=_F32)


def compress(hb, pos, w1, b1, w2):
    G, nh, half = hb.shape
    posflat = pos.reshape(1, CMP_BLOCK * HEAD_DIM)
    return pl.pallas_call(
        _compress_kernel,
        grid=(G,),
        in_specs=[pl.BlockSpec((1, nh, half), lambda g: (g, 0, 0)),
                  pl.BlockSpec((1, 2 * half), lambda g: (0, 0)),
                  pl.BlockSpec((2 * half, CMP_HIDDEN), lambda g: (0, 0)),
                  pl.BlockSpec((1, CMP_HIDDEN), lambda g: (0, 0)),
                  pl.BlockSpec((CMP_HIDDEN, HEAD_DIM), lambda g: (0, 0))],
        out_specs=pl.BlockSpec((1, nh, HEAD_DIM), lambda g: (g, 0, 0)),
        out_shape=jax.ShapeDtypeStruct((G, nh, HEAD_DIM), _F32),
        compiler_params=_cparams(("arbitrary",)),
        name="compress",
    )(hb, posflat, w1.astype(_BF16), b1.reshape(1, CMP_HIDDEN), w2.astype(_BF16))


def _nsa_kernel(q_ref, kc_ref, vct_ref, ks_ref, vst_ref, kw_ref, vwt_ref, gl_ref, slope_ref, cb_ref, rb_ref,
                ovt_ref, o_ref, sel_ref, m_ref, l_ref, acc_ref, flag_ref, *, n_sel):
    nb = pl.program_id(1)
    s0 = nb * Q_BLOCK
    R = HEADS_PER_GROUP
    L = R * Q_BLOCK
    nh = kc_ref.shape[1]
    n_slc = ovt_ref.shape[0]
    scale = HEAD_DIM ** -0.5

    qs = jnp.concatenate([q_ref[:, r * HEAD_DIM:(r + 1) * HEAD_DIM] for r in range(R)], axis=0)
    qs = (qs * scale).astype(_BF16)
    slope = slope_ref[0]
    lane = lax.broadcasted_iota(jnp.int32, (1, L), 1)
    t_i = s0 + (lane & (Q_BLOCK - 1))

    sc = lax.dot_general(kc_ref[0], qs, _NT, preferred_element_type=_F32)
    n_io = lax.broadcasted_iota(jnp.int32, (nh, L), 0)
    valid = n_io <= ((t_i - (CMP_BLOCK - 1)) >> CMP_STRIDE_LOG2)
    s = jnp.where(valid, sc + cb_ref[0], NEG_INF)
    m = jnp.max(s, axis=0, keepdims=True)
    e = jnp.where(valid, jnp.exp(s - m), 0.0)
    l = jnp.sum(e, axis=0, keepdims=True)
    p = e * (1.0 / jnp.maximum(l, 1e-30))
    o_cmp = jnp.dot(vct_ref[0], p.astype(_BF16), preferred_element_type=_F32)

    ps = p[:, 0:Q_BLOCK]
    for r in range(1, R):
        ps = ps + p[:, r * Q_BLOCK:(r + 1) * Q_BLOCK]
    hi = ps.astype(_BF16)
    r1 = ps - hi.astype(_F32)
    mid = r1.astype(_BF16)
    lo = (r1 - mid.astype(_F32)).astype(_BF16)
    ovt = ovt_ref[...]
    imp = (jnp.dot(ovt, hi, preferred_element_type=_F32) + jnp.dot(ovt, mid, preferred_element_type=_F32)
           + jnp.dot(ovt, lo, preferred_element_type=_F32))

    j_io = lax.broadcasted_iota(jnp.int32, (n_slc, Q_BLOCK), 0)
    tq = s0 + lax.broadcasted_iota(jnp.int32, (1, Q_BLOCK), 1)
    cur = tq >> SLC_BLOCK_LOG2
    valid_s = (j_io * SLC_BLOCK) <= tq
    forced = (j_io == 0) | (j_io == cur) | (j_io == cur - 1)
    score = jnp.where(valid_s, imp + jnp.where(forced, FORCE_BONUS, 0.0), -1.0)

    def pick(_, carry):
        work, sel = carry
        mx = jnp.max(work, axis=0, keepdims=True)
        first = jnp.min(jnp.where(work == mx, j_io, n_slc), axis=0, keepdims=True)
        hit = j_io == first
        return jnp.where(hit, -jnp.inf, work), jnp.where(hit, 1.0, sel)

    _, sel = lax.fori_loop(0, n_sel, pick, (score, jnp.zeros((n_slc, Q_BLOCK), _F32)))
    sel_ref[...] = sel
    for k in range(n_slc // SLC_PER_TILE):
        flag_ref[k] = (jnp.max(sel[k * SLC_PER_TILE:(k + 1) * SLC_PER_TILE, :]) > 0.0).astype(jnp.int32)

    m_ref[...] = jnp.full((1, L), NEG_INF, _F32)
    l_ref[...] = jnp.zeros((1, L), _F32)
    acc_ref[...] = jnp.zeros((HEAD_DIM, L), _F32)
    key_io = lax.broadcasted_iota(jnp.int32, (SLC_TILE, L), 0)
    last = (s0 + Q_BLOCK - 1) // SLC_TILE

    def tile_update(kt, causal):
        selt = sel_ref[pl.ds(pl.multiple_of(kt * SLC_PER_TILE, SLC_PER_TILE), SLC_PER_TILE), :]
        sx = jnp.broadcast_to(selt[:, None, :], (SLC_PER_TILE, SLC_BLOCK, Q_BLOCK)).reshape(SLC_TILE, Q_BLOCK)
        ok = jnp.concatenate([sx] * R, axis=1) > 0.0
        if causal:
            ok = ok & (key_io <= t_i - kt * SLC_TILE)
        sr = lax.dot_general(ks_ref[0, kt], qs, _NT, preferred_element_type=_F32)
        st = jnp.where(ok, sr + rb_ref[0, 0:SLC_TILE, :], NEG_INF)
        ck = slope * (kt * SLC_TILE).astype(_F32)
        m_old = m_ref[...]
        m_new = jnp.maximum(m_old, jnp.max(st, axis=0, keepdims=True) + ck)
        alpha = jnp.exp(m_old - m_new)
        et = jnp.exp(st - (m_new - ck))
        l_ref[...] = alpha * l_ref[...] + jnp.sum(et, axis=0, keepdims=True)
        acc_ref[...] = alpha * acc_ref[...] + jnp.dot(vst_ref[0, kt], et.astype(_BF16),
                                                      preferred_element_type=_F32)
        m_ref[...] = m_new

    def slc_tile(kt, carry):
        @pl.when(flag_ref[kt] > 0)
        def _():
            tile_update(kt, False)
        return carry

    lax.fori_loop(0, last, slc_tile, 0)
    tile_update(last, True)
    o_slc = acc_ref[...] * (1.0 / l_ref[...])

    start = pl.multiple_of(jnp.maximum(s0 - WINDOW, 0), Q_BLOCK)
    sw = lax.dot_general(kw_ref[0, pl.ds(start, WIN_KEYS), :], qs, _NT, preferred_element_type=_F32)
    row_io = lax.broadcasted_iota(jnp.int32, (WIN_KEYS, L), 0)
    rel = t_i - start
    okw = (row_io <= rel) & (row_io > rel - WINDOW)
    s = jnp.where(okw, sw + rb_ref[0], NEG_INF)
    m = jnp.max(s, axis=0, keepdims=True)
    e = jnp.exp(s - m)
    pw = (e * (1.0 / jnp.sum(e, axis=0, keepdims=True))).astype(_BF16)
    b0 = start // Q_BLOCK
    o_win = jnp.dot(vwt_ref[0, b0], pw[0:Q_BLOCK], preferred_element_type=_F32)
    for i in range(1, WIN_BLOCKS):
        o_win = o_win + jnp.dot(vwt_ref[0, b0 + i], pw[i * Q_BLOCK:(i + 1) * Q_BLOCK],
                                preferred_element_type=_F32)

    sg = jax.nn.sigmoid(gl_ref[0, 0])
    o = sg[0:1] * o_cmp + sg[1:2] * o_slc + sg[2:3] * o_win
    for r in range(R):
        o_ref[:, r * HEAD_DIM:(r + 1) * HEAD_DIM] = o[:, r * Q_BLOCK:(r + 1) * Q_BLOCK].T


def nsa(proj, projb, kc, vc):
    S = proj.shape[0]
    G, R = N_KV_GROUPS, HEADS_PER_GROUP
    nh = S // CMP_STRIDE
    n_slc = S // SLC_BLOCK
    n_qb = S // Q_BLOCK
    n_kt = S // SLC_TILE
    n_sel = min(N_SELECT, n_slc)
    assert n_sel >= 3 and S >= WIN_KEYS
    L = R * Q_BLOCK

    def grp(off):
        return projb[:, off:off + KV_WIDTH].reshape(S, G, HEAD_DIM).transpose(1, 0, 2)

    kcb = kc.astype(_BF16)
    vct = vc.transpose(0, 2, 1).astype(_BF16)
    ks3 = grp(OFF_KS).reshape(G, n_kt, SLC_TILE, HEAD_DIM)
    vst3 = grp(OFF_VS).reshape(G, n_kt, SLC_TILE, HEAD_DIM).transpose(0, 1, 3, 2)
    kw = grp(OFF_KW)
    vwt3 = grp(OFF_VW).reshape(G, n_qb, Q_BLOCK, HEAD_DIM).transpose(0, 1, 3, 2)
    gl = proj[:, OFF_GATE:OFF_GATE + N_ATT_HEADS * N_BRANCH].reshape(n_qb, Q_BLOCK, G, R, N_BRANCH)
    gl = gl.transpose(2, 0, 4, 3, 1).reshape(G, n_qb, N_BRANCH, L)
    head = np.arange(N_ATT_HEADS, dtype=np.float64).reshape(G, R)
    slopes = np.repeat(2.0 ** (-8.0 * (head + 1) / N_ATT_HEADS), Q_BLOCK, axis=1).reshape(G, 1, L)
    slopes = jnp.asarray(slopes, _F32)
    cpos = jnp.arange(nh, dtype=_F32) * CMP_STRIDE + (CMP_BLOCK - 1) / 2.0
    cmp_bias = slopes * cpos[None, :, None]
    row_bias = slopes * jnp.arange(WIN_KEYS, dtype=_F32)[None, :, None]
    ci = np.arange(nh)[None, :] * CMP_STRIDE
    sj = np.arange(n_slc)[:, None] * SLC_BLOCK
    ovt = ((ci < sj + SLC_BLOCK) & (ci + CMP_BLOCK > sj) & (np.arange(nh)[None, :] < nh - 1))

    return pl.pallas_call(
        functools.partial(_nsa_kernel, n_sel=n_sel),
        grid=(G, n_qb),
        in_specs=[
            pl.BlockSpec((Q_BLOCK, L), lambda g, i: (i, g)),
            pl.BlockSpec((1, nh, HEAD_DIM), lambda g, i: (g, 0, 0)),
            pl.BlockSpec((1, HEAD_DIM, nh), lambda g, i: (g, 0, 0)),
            pl.BlockSpec((1, n_kt, SLC_TILE, HEAD_DIM), lambda g, i: (g, 0, 0, 0)),
            pl.BlockSpec((1, n_kt, HEAD_DIM, SLC_TILE), lambda g, i: (g, 0, 0, 0)),
            pl.BlockSpec((1, S, HEAD_DIM), lambda g, i: (g, 0, 0)),
            pl.BlockSpec((1, n_qb, HEAD_DIM, Q_BLOCK), lambda g, i: (g, 0, 0, 0)),
            pl.BlockSpec((1, 1, N_BRANCH, L), lambda g, i: (g, i, 0, 0)),
            pl.BlockSpec((1, 1, L), lambda g, i: (g, 0, 0)),
            pl.BlockSpec((1, nh, L), lambda g, i: (g, 0, 0)),
            pl.BlockSpec((1, WIN_KEYS, L), lambda g, i: (g, 0, 0)),
            pl.BlockSpec((n_slc, nh), lambda g, i: (0, 0)),
        ],
        out_specs=pl.BlockSpec((Q_BLOCK, L), lambda g, i: (i, g)),
        out_shape=jax.ShapeDtypeStruct((S, ATT_WIDTH), _F32),
        scratch_shapes=[pltpu.VMEM((n_slc, Q_BLOCK), _F32),
                        pltpu.VMEM((1, L), _F32),
                        pltpu.VMEM((1, L), _F32),
                        pltpu.VMEM((HEAD_DIM, L), _F32),
                        pltpu.SMEM((n_kt,), jnp.int32)],
        compiler_params=_cparams(("arbitrary", "arbitrary")),
        name="nsa",
    )(proj, kcb, vct, ks3, vst3, kw, vwt3, gl, slopes, cmp_bias, row_bias, jnp.asarray(ovt, _BF16))


def _layernorm_rows(r, g, b):
    mu = jnp.mean(r, axis=-1, keepdims=True)
    c = r - mu
    var = jnp.mean(c * c, axis=-1, keepdims=True)
    return c * lax.rsqrt(var + LN_EPS) * g + b


def _mix_out_kernel(x_ref, att_ref, u_ref, gb_ref, gc_ref, uh_ref, gch_ref, cw_ref, hg_ref, wo_ref,
                    g1_ref, b1_ref, h_ref):
    i = pl.program_id(0)
    tm = x_ref.shape[0]
    z = gc_ref[...] * u_ref[...]
    zh = jnp.where(i > 0, gch_ref[...] * uh_ref[...], 0.0)
    zz = jnp.concatenate([zh, z], axis=0)
    z1 = pltpu.roll(zz, 1, 0)[SUBLANES:]
    z2 = pltpu.roll(zz, 2, 0)[SUBLANES:]
    y = cw_ref[0:1, :] * z2 + cw_ref[1:2, :] * z1 + cw_ref[2:3, :] * z
    o_conv = gb_ref[...] * y
    mix = jnp.concatenate([att_ref[...], o_conv], axis=1)
    parts = []
    for k in range(N_MIX_GROUPS):
        blk = mix[:, k * HEAD_DIM:(k + 1) * HEAD_DIM]
        ms = jnp.mean(blk * blk, axis=-1, keepdims=True)
        parts.append(blk * lax.rsqrt(ms + RMS_EPS))
    mixn = (jnp.concatenate(parts, axis=1) * hg_ref[...]).astype(_BF16)
    r = ALPHA * x_ref[...] + jnp.dot(mixn, wo_ref[...], preferred_element_type=_F32)
    h_ref[...] = _layernorm_rows(r, g1_ref[...], b1_ref[...])


def mix_out(x2, o_att, proj, conv_w, head_norm_g, w_out, ln_g, ln_b):
    S = x2.shape[0]
    tm = min(256, S)
    hb = tm // SUBLANES
    cblk = lambda off: off // CONV_WIDTH
    halo = lambda off: pl.BlockSpec((SUBLANES, CONV_WIDTH), lambda i: (jnp.maximum(i * hb - 1, 0), cblk(off)))
    tile = lambda off: pl.BlockSpec((tm, CONV_WIDTH), lambda i: (i, cblk(off)))
    full = lambda shape: pl.BlockSpec(shape, lambda i: (0, 0))
    cw = jnp.zeros((SUBLANES, CONV_WIDTH), _F32).at[:CONV_K].set(conv_w)
    return pl.pallas_call(
        _mix_out_kernel,
        grid=(S // tm,),
        in_specs=[pl.BlockSpec((tm, D_MODEL), lambda i: (i, 0)),
                  pl.BlockSpec((tm, ATT_WIDTH), lambda i: (i, 0)),
                  tile(OFF_U), tile(OFF_GB), tile(OFF_GC), halo(OFF_U), halo(OFF_GC),
                  full((SUBLANES, CONV_WIDTH)), full((1, MIX_WIDTH)), full((MIX_WIDTH, D_MODEL)),
                  full((1, D_MODEL)), full((1, D_MODEL))],
        out_specs=pl.BlockSpec((tm, D_MODEL), lambda i: (i, 0)),
        out_shape=jax.ShapeDtypeStruct((S, D_MODEL), _F32),
        compiler_params=_cparams(("arbitrary",)),
        name="mix_out",
    )(x2, o_att, proj, proj, proj, proj, proj, cw, head_norm_g.reshape(1, MIX_WIDTH), w_out.astype(_BF16),
      ln_g.reshape(1, D_MODEL), ln_b.reshape(1, D_MODEL))


def _topk_rows(x, k):
    R, L = x.shape
    io = lax.broadcasted_iota(jnp.int32, (R, L), 0)
    ko = lax.broadcasted_iota(jnp.int32, (k, L), 0)
    vals = jnp.zeros((k, L), _F32)
    idxs = jnp.zeros((k, L), jnp.int32)
    for i in range(k):
        mx = jnp.max(x, axis=0, keepdims=True)
        first = jnp.min(jnp.where(x == mx, io, R), axis=0, keepdims=True)
        x = jnp.where(io == first, -jnp.inf, x)
        vals = jnp.where(ko == i, mx, vals)
        idxs = jnp.where(ko == i, first, idxs)
    return vals, idxs


def _take_rows(table, idx):
    out = jnp.zeros(idx.shape, table.dtype)
    for a in range(table.shape[0]):
        out = jnp.where(idx == a, table[a:a + 1, :], out)
    return out


def _peer_route_kernel(h_ref, wq_ref, sk_ref, eid_ref, g_ref):
    qp = jnp.dot(h_ref[...].astype(_BF16), wq_ref[...], preferred_element_type=_F32)
    for hd in range(PEER_HEADS):
        tops = []
        for c in range(2):
            j = hd * 2 + c
            qhc = qp[:, j * PEER_HALF:(j + 1) * PEER_HALF].astype(_BF16)
            st = lax.dot_general(sk_ref[j], qhc, _NT, preferred_element_type=_F32)
            tops.append(_topk_rows(st, PEER_TOPK))
        (v1, i1), (v2, i2) = tops
        nb = [PEER_TOPK // (a + 1) for a in range(PEER_TOPK)]
        tt = i1.shape[1]
        pad = -sum(nb) % SUBLANES
        cand = jnp.concatenate([v1[a:a + 1, :] + v2[0:nb[a], :] for a in range(PEER_TOPK)]
                               + [jnp.full((pad, tt), -jnp.inf, _F32)], axis=0)
        bs, bi = _topk_rows(cand, PEER_TOPK)
        zpad = [jnp.zeros((pad, tt), jnp.int32)]
        t1 = jnp.concatenate([jnp.broadcast_to(i1[a:a + 1, :], (nb[a], tt)) for a in range(PEER_TOPK)] + zpad,
                             axis=0)
        t2 = jnp.concatenate([i2[0:nb[a], :] for a in range(PEER_TOPK)] + zpad, axis=0)
        e1 = _take_rows(t1, bi)
        e2 = _take_rows(t2, bi)
        ex = jnp.exp(bs - jnp.max(bs, axis=0, keepdims=True))
        eid_ref[hd * PEER_TOPK:(hd + 1) * PEER_TOPK, :] = e1 * N_KEYS + e2
        g_ref[hd * PEER_TOPK:(hd + 1) * PEER_TOPK, :] = ex * (1.0 / jnp.sum(ex, axis=0, keepdims=True))


def peer_route(h, w_query, sub_keys):
    S = h.shape[0]
    tt = LANES
    skb = sub_keys.reshape(PEER_HEADS * 2, N_KEYS, PEER_HALF).astype(_BF16)
    return pl.pallas_call(
        _peer_route_kernel,
        grid=(S // tt,),
        in_specs=[pl.BlockSpec((tt, D_MODEL), lambda i: (i, 0)),
                  pl.BlockSpec((D_MODEL, PEER_HEADS * 2 * PEER_HALF), lambda i: (0, 0)),
                  pl.BlockSpec((PEER_HEADS * 2, N_KEYS, PEER_HALF), lambda i: (0, 0, 0))],
        out_specs=[pl.BlockSpec((PEER_K, tt), lambda i: (0, i)),
                   pl.BlockSpec((PEER_K, tt), lambda i: (0, i))],
        out_shape=[jax.ShapeDtypeStruct((PEER_K, S), jnp.int32),
                   jax.ShapeDtypeStruct((PEER_K, S), _F32)],
        compiler_params=_cparams(("arbitrary",)),
        name="peer_route",
    )(h, w_query.astype(_BF16), skb)


def _row_copy(uv_hbm, buf, sem, e, r):
    return pltpu.make_async_copy(uv_hbm.at[e], buf.at[r], sem)


def _wait_all_rows(buf, sem):
    pltpu.make_async_copy(buf, buf, sem).wait()


def _dot_f32_rows(x, w):
    T = x.shape[0]
    hi = x.astype(w.dtype)
    r1 = x - hi.astype(_F32)
    mid = r1.astype(w.dtype)
    lo = (r1 - mid.astype(_F32)).astype(w.dtype)
    y = jnp.dot(jnp.concatenate([hi, mid, lo], axis=0), w, preferred_element_type=_F32)
    return y[0:T] + y[T:2 * T] + y[2 * T:3 * T]


def _peer_batch(src, issue, h_ref, o_ref, t0, g, gsum, gexp, g2, b2, h3_ref, c_ref, r3_ref):
    wide = PEER_K * D_CHUNKS
    diag = (lax.broadcasted_iota(jnp.int32, (D_CHUNKS, wide), 1) & (D_CHUNKS - 1)) == \
        lax.broadcasted_iota(jnp.int32, (D_CHUNKS, wide), 0)
    n_it = PEER_TB // PEER_TPB
    for t in range(PEER_TB):
        h3_ref[t] = jnp.concatenate([h_ref[t0 + t:t0 + t + 1, s * LANES:(s + 1) * LANES]
                                     for s in range(D_CHUNKS)], axis=0)

    def u_side(it, carry):
        issue(it)
        for j in range(PEER_TPB):
            t = it * PEER_TPB + j
            ut = src[pl.ds(t * PEER_K, PEER_K), 0:D_CHUNKS, :].reshape(wide, LANES)
            y = lax.dot_general(h3_ref[t].astype(_BF16), ut, _NT, preferred_element_type=_F32)
            c_ref[pl.ds(t, 1), :] = jnp.sum(jnp.where(diag, y, 0.0), axis=0, keepdims=True)
        return carry

    lax.fori_loop(0, n_it, u_side, 0)
    a = _dot_f32_rows(c_ref[...], gsum)
    c = (jax.nn.gelu(a) * g).astype(_BF16)
    c_ref[...] = jnp.dot(c, gexp, preferred_element_type=_F32)

    def v_side(it, carry):
        issue(n_it + it)
        for j in range(PEER_TPB):
            t = it * PEER_TPB + j
            lt = jnp.where(diag, c_ref[pl.ds(t, 1), :], 0.0).astype(_BF16)
            vt = src[pl.ds(t * PEER_K, PEER_K), D_CHUNKS:2 * D_CHUNKS, :].reshape(wide, LANES)
            r3_ref[t] = ALPHA * h3_ref[t] + jnp.dot(lt, vt, preferred_element_type=_F32)
        return carry

    lax.fori_loop(0, n_it, v_side, 0)
    for t in range(PEER_TB):
        r = r3_ref[t]
        mu = jnp.mean(r, axis=(0, 1), keepdims=True)
        d = r - mu
        var = jnp.mean(d * d, axis=(0, 1), keepdims=True)
        y = d * lax.rsqrt(var + LN_EPS) * g2 + b2
        for s in range(D_CHUNKS):
            o_ref[t0 + t:t0 + t + 1, s * LANES:(s + 1) * LANES] = y[s:s + 1, :]


def _peer_apply_kernel(eid_cur_ref, eid_nxt_ref, h_ref, g_ref, gs_ref, ex_ref, g2_ref, b2_ref, uv_hbm, o_ref,
                       buf_a, buf_b, sem, h3_ref, c_ref, r3_ref):
    i = pl.program_id(0)
    n = pl.num_programs(0)
    n_parts = 2 * (PEER_TB // PEER_TPB)
    per_part = PEER_ROWS // n_parts

    def issuer(eid_ref, e0, buf, s):
        def issue(part):
            base = pl.multiple_of(part * per_part, per_part)
            rows = buf.at[pl.ds(base, per_part)]
            for j in range(per_part):
                _row_copy(uv_hbm, rows, s, eid_ref[0, 0, e0 + base + j], j).start(priority=j % 2)
        return issue

    @pl.when(i == 0)
    def _():
        first = issuer(eid_cur_ref, 0, buf_a, sem.at[0])

        def body(part, carry):
            first(part)
            return carry
        lax.fori_loop(0, n_parts, body, 0)

    _wait_all_rows(buf_a, sem.at[0])
    _peer_batch(buf_a, issuer(eid_cur_ref, PEER_ROWS, buf_b, sem.at[1]), h_ref, o_ref, 0, g_ref[0:PEER_TB, :],
                gs_ref[...], ex_ref[...], g2_ref[...], b2_ref[...], h3_ref, c_ref, r3_ref)

    _wait_all_rows(buf_b, sem.at[1])
    _peer_batch(buf_b, issuer(eid_nxt_ref, 0, buf_a, sem.at[0]), h_ref, o_ref, PEER_TB,
                g_ref[PEER_TB:2 * PEER_TB, :], gs_ref[...], ex_ref[...], g2_ref[...], b2_ref[...],
                h3_ref, c_ref, r3_ref)

    @pl.when(i == n - 1)
    def _():
        _wait_all_rows(buf_a, sem.at[0])


def pack_expert_rows(expert_u, expert_v):
    E = expert_u.shape[0]
    tile = lambda w: w.reshape(E, D_CHUNKS, LANES)
    return jnp.concatenate([tile(expert_u), tile(expert_v)], axis=1).astype(_BF16)


def peer_apply(h, eid, gates, uvp, ln_g, ln_b):
    S = h.shape[0]
    npair = S // (2 * PEER_TB)
    eid3 = eid.reshape(npair, 1, 2 * PEER_ROWS)
    wide = PEER_K * D_CHUNKS
    expand = (np.arange(wide)[None, :] // D_CHUNKS) == np.arange(PEER_K)[:, None]
    gbuf = pltpu.VMEM((PEER_ROWS, 2 * D_CHUNKS, LANES), uvp.dtype)
    return pl.pallas_call(
        _peer_apply_kernel,
        grid=(npair,),
        in_specs=[
            pl.BlockSpec((1, 1, 2 * PEER_ROWS), lambda i: (i, 0, 0), memory_space=pltpu.SMEM),
            pl.BlockSpec((1, 1, 2 * PEER_ROWS), lambda i: (jnp.minimum(i + 1, npair - 1), 0, 0),
                         memory_space=pltpu.SMEM),
            pl.BlockSpec((2 * PEER_TB, D_MODEL), lambda i: (i, 0)),
            pl.BlockSpec((2 * PEER_TB, PEER_K), lambda i: (i, 0)),
            pl.BlockSpec((wide, PEER_K), lambda i: (0, 0)),
            pl.BlockSpec((PEER_K, wide), lambda i: (0, 0)),
            pl.BlockSpec((D_CHUNKS, LANES), lambda i: (0, 0)),
            pl.BlockSpec((D_CHUNKS, LANES), lambda i: (0, 0)),
            pl.BlockSpec(memory_space=pl.ANY),
        ],
        out_specs=pl.BlockSpec((2 * PEER_TB, D_MODEL), lambda i: (i, 0)),
        out_shape=jax.ShapeDtypeStruct((S, D_MODEL), _F32),
        scratch_shapes=[gbuf, gbuf, pltpu.SemaphoreType.DMA((2,)),
                        pltpu.VMEM((PEER_TB, D_CHUNKS, LANES), _F32),
                        pltpu.VMEM((PEER_TB, wide), _F32),
                        pltpu.VMEM((PEER_TB, D_CHUNKS, LANES), _F32)],
        compiler_params=_cparams(("arbitrary",)),
        name="peer_apply",
    )(eid3, eid3, h, gates, jnp.asarray(expand.T, _BF16), jnp.asarray(expand, _BF16),
      ln_g.reshape(D_CHUNKS, LANES), ln_b.reshape(D_CHUNKS, LANES), uvp)


def _pad_w_in(w):
    offs = np.cumsum([0, ATT_WIDTH] + [KV_WIDTH] * 6 + [N_ATT_HEADS * N_BRANCH] + [CONV_WIDTH] * 3)
    seg = lambda k: w[:, offs[k]:offs[k + 1]]
    q, kc, vc, ks, vs, kw, vw, gate, u, gb, gc = [seg(k) for k in range(11)]
    gate = jnp.pad(gate, ((0, 0), (0, GATE_PAD - gate.shape[1])))
    return jnp.concatenate([q, u, gb, gc, kc, vc, ks, vs, kw, vw, gate], axis=1)


def _layer(x2, w_in, cmp_k, cmp_v, conv_w, head_norm_g, w_out, ln1, w_query, sub_keys, expert_u, expert_v, ln2):
    S = x2.shape[0]
    G = N_KV_GROUPS
    proj, projb = in_proj(x2.astype(_BF16), _pad_w_in(w_in).astype(_BF16))

    def half_blocks(off):
        t = proj[:, off:off + KV_WIDTH].reshape(S, G, HEAD_DIM).transpose(1, 0, 2)
        return t.reshape(G, S // CMP_STRIDE, CMP_STRIDE * HEAD_DIM)

    kc = compress(half_blocks(OFF_KC), *cmp_k)
    vc = compress(half_blocks(OFF_VC), *cmp_v)
    o_att = nsa(proj, projb, kc, vc)
    h = mix_out(x2, o_att, proj, conv_w, head_norm_g, w_out, *ln1)
    eid_t, g_t = peer_route(h, w_query, sub_keys)
    return peer_apply(h, eid_t.T, g_t.T, pack_expert_rows(expert_u, expert_v), *ln2)


def kernel(x, w_in, cmp_k_pos, cmp_k_w1, cmp_k_b1, cmp_k_w2, cmp_v_pos, cmp_v_w1, cmp_v_b1, cmp_v_w2, conv_w, head_norm_g, w_out, ln1_g, ln1_b, w_query, sub_keys, expert_u, expert_v, ln2_g, ln2_b):
    B, S, D = x.shape
    outs = []
    for b in range(B):
        xb = x[b]
        for l in range(w_in.shape[0]):
            xb = _layer(xb, w_in[l],
                        (cmp_k_pos[l], cmp_k_w1[l], cmp_k_b1[l], cmp_k_w2[l]),
                        (cmp_v_pos[l], cmp_v_w1[l], cmp_v_b1[l], cmp_v_w2[l]),
                        conv_w[l], head_norm_g[l], w_out[l], (ln1_g[l], ln1_b[l]),
                        w_query[l], sub_keys[l], expert_u[l], expert_v[l], (ln2_g[l], ln2_b[l]))
        outs.append(xb)
    return jnp.stack(outs, axis=0)
```
